```python
import math
import jax, jax.numpy as jnp
from jax import lax
import numpy as np

D_MODEL = 1024
BATCH = 8
SEQ = 8192
DEPTH = 1

BRANCH_W = D_MODEL
N_BRANCH = 3
LRU_W = BRANCH_W
LRU_BLOCKS = 16
LRU_BLOCK_W = LRU_W // LRU_BLOCKS
LRU_C = 8.0
CONV_W = 4
FOX_HEAD_DIM = 64
FOX_HEADS = BRANCH_W // FOX_HEAD_DIM
FOX_W = FOX_HEADS * FOX_HEAD_DIM
Q_BLOCK = 128
MEM_LEN = 256
MEM_HEADS = 4
MEM_HEAD_DIM = BRANCH_W // MEM_HEADS
MEM_W = MEM_HEADS * MEM_HEAD_DIM
RMS_EPS = 1e-6
NEG_INF = -1e30
SPLIT_SIZES = (LRU_W, LRU_W, FOX_W, FOX_W, FOX_W, FOX_HEADS, FOX_W, MEM_W, MEM_W, N_BRANCH * D_MODEL)
IN_COLS = sum(SPLIT_SIZES)

kernel_name = "hybrid_rglru_fox_memxattn_gated_merge"


def rmsnorm(x, g):
    xf = x.astype(jnp.float32)
    y = xf * lax.rsqrt(jnp.mean(xf * xf, axis=-1, keepdims=True) + RMS_EPS)
    return (y * g.astype(jnp.float32)).astype(x.dtype)


def causal_depthwise_conv(u, w, b):
    C = u.shape[-1]
    out = lax.conv_general_dilated(
        u, w.reshape(CONV_W, 1, C).astype(u.dtype),
        window_strides=(1,), padding=[(CONV_W - 1, 0)],
        dimension_numbers=("NWC", "WIO", "NWC"), feature_group_count=C)
    return out + b.astype(u.dtype)


def rg_lru(xc, w_r, b_r, w_i, b_i, lam):
    B, S, W = xc.shape
    xg = xc.reshape(B, S, LRU_BLOCKS, LRU_BLOCK_W)
    r = jax.nn.sigmoid(jnp.einsum("bsgi,gij->bsgj", xg, w_r) + b_r).reshape(B, S, W)
    i = jax.nn.sigmoid(jnp.einsum("bsgi,gij->bsgj", xg, w_i) + b_i).reshape(B, S, W)
    log_a = (-LRU_C * r.astype(jnp.float32)) * jax.nn.softplus(-lam.astype(jnp.float32))
    a = jnp.exp(log_a)
    mult = jnp.sqrt(-jnp.expm1(2.0 * log_a))
    u = mult * (i * xc).astype(jnp.float32)

    def combine(left, right):
        a1, b1 = left
        a2, b2 = right
        return a1 * a2, a2 * b1 + b2

    _, h = lax.associative_scan(combine, (a, u), axis=1)
    return h.astype(xc.dtype)


def forgetting_attention(q, k, v, log_f):
    B, S, H, Dh = q.shape
    nb = S // Q_BLOCK
    scale = 1.0 / math.sqrt(Dh)
    c = jnp.cumsum(log_f, axis=1)
    c_k = jnp.transpose(c, (0, 2, 1))
    kf = k.astype(jnp.float32)
    vf = v.astype(jnp.float32)
    qb = jnp.transpose(q.reshape(B, nb, Q_BLOCK, H, Dh), (1, 0, 2, 3, 4))
    cb = jnp.transpose(c.reshape(B, nb, Q_BLOCK, H), (1, 0, 3, 2))
    kpos = jnp.arange(S)

    def one_block(args):
        blk, qi, ci = args
        s = jnp.einsum("bqhd,bkhd->bhqk", qi.astype(jnp.float32), kf) * scale
        s = s + (ci[..., :, None] - c_k[..., None, :])
        qpos = blk * Q_BLOCK + jnp.arange(Q_BLOCK)
        mask = kpos[None, :] <= qpos[:, None]
        s = jnp.where(mask, s, NEG_INF)
        p = jax.nn.softmax(s, axis=-1)
        return jnp.einsum("bhqk,bkhd->bqhd", p, vf)

    o = lax.map(one_block, (jnp.arange(nb), qb, cb))
    return jnp.transpose(o, (1, 0, 2, 3, 4)).reshape(B, S, H, Dh).astype(q.dtype)


def memory_attention(qm, mk, mv):
    scale = 1.0 / math.sqrt(qm.shape[-1])
    s = jnp.einsum("bshd,bmhd->bhsm", qm.astype(jnp.float32), mk.astype(jnp.float32)) * scale
    p = jax.nn.softmax(s, axis=-1)
    o = jnp.einsum("bhsm,bmhd->bshd", p, mv.astype(jnp.float32))
    return o.astype(qm.dtype)


def setup_inputs(seed: int = 0) -> dict:
    key = jax.random.key(seed)
    ks = jax.random.split(key, 20)
    f32 = jnp.float32
    x = jax.random.normal(ks[0], (BATCH, SEQ, D_MODEL), f32)
    mem = jax.random.normal(ks[1], (BATCH, MEM_LEN, D_MODEL), f32)
    g_pre = 1.0 + 0.1 * jax.random.normal(ks[2], (D_MODEL,), f32)
    w_in = jax.random.normal(ks[3], (D_MODEL, IN_COLS), f32) * D_MODEL ** -0.5
    conv_w = jax.random.normal(ks[4], (CONV_W, LRU_W), f32) * CONV_W ** -0.5
    conv_b = 0.02 * jax.random.normal(ks[5], (LRU_W,), f32)
    w_lru_r = jax.random.normal(ks[6], (LRU_BLOCKS, LRU_BLOCK_W, LRU_BLOCK_W), f32) * LRU_BLOCK_W ** -0.5
    b_lru_r = 0.02 * jax.random.normal(ks[7], (LRU_BLOCKS, LRU_BLOCK_W), f32)
    w_lru_i = jax.random.normal(ks[8], (LRU_BLOCKS, LRU_BLOCK_W, LRU_BLOCK_W), f32) * LRU_BLOCK_W ** -0.5
    b_lru_i = 0.02 * jax.random.normal(ks[9], (LRU_BLOCKS, LRU_BLOCK_W), f32)
    a0 = jax.random.uniform(ks[10], (LRU_W,), f32, minval=0.9, maxval=0.999)
    lru_lambda = jnp.log(a0) - jnp.log1p(-a0)
    b_forget = 3.0 + 0.5 * jax.random.normal(ks[11], (FOX_HEADS,), f32)
    g_mem = 1.0 + 0.1 * jax.random.normal(ks[12], (D_MODEL,), f32)
    w_mem_k = jax.random.normal(ks[13], (D_MODEL, MEM_W), f32) * D_MODEL ** -0.5
    w_mem_v = jax.random.normal(ks[14], (D_MODEL, MEM_W), f32) * D_MODEL ** -0.5
    w_branch = jax.random.normal(ks[15], (N_BRANCH, BRANCH_W, D_MODEL), f32) * BRANCH_W ** -0.5
    b_merge = 0.02 * jax.random.normal(ks[16], (N_BRANCH, D_MODEL), f32)
    w_out = jax.random.normal(ks[17], (D_MODEL, D_MODEL), f32) * D_MODEL ** -0.5
    g_post = 1.0 + 0.1 * jax.random.normal(ks[18], (D_MODEL,), f32)
    return {"x": x, "mem": mem, "g_pre": g_pre, "w_in": w_in, "conv_w": conv_w, "conv_b": conv_b,
            "w_lru_r": w_lru_r, "b_lru_r": b_lru_r, "w_lru_i": w_lru_i, "b_lru_i": b_lru_i,
            "lru_lambda": lru_lambda, "b_forget": b_forget, "g_mem": g_mem, "w_mem_k": w_mem_k,
            "w_mem_v": w_mem_v, "w_branch": w_branch, "b_merge": b_merge, "w_out": w_out,
            "g_post": g_post}


def hybrid_layer(x, mem, g_pre, w_in, conv_w, conv_b, w_lru_r, b_lru_r, w_lru_i, b_lru_i,
                 lru_lambda, b_forget, g_mem, w_mem_k, w_mem_v, w_branch, b_merge, w_out, g_post):
    B, S, D = x.shape
    Bm, M, _ = mem.shape
    xn = rmsnorm(x, g_pre)
    z = xn @ w_in
    offsets = [int(o) for o in np.cumsum(SPLIT_SIZES)[:-1]]
    (a_x, a_gate, f_q, f_k, f_v, f_logit, f_gate, m_q, m_gate, merge_logit) = jnp.split(z, offsets, axis=-1)

    xc = causal_depthwise_conv(a_x, conv_w, conv_b)
    h = rg_lru(xc, w_lru_r, b_lru_r, w_lru_i, b_lru_i, lru_lambda)
    y_a = h * jax.nn.silu(a_gate)

    q = f_q.reshape(B, S, FOX_HEADS, FOX_HEAD_DIM)
    k = f_k.reshape(B, S, FOX_HEADS, FOX_HEAD_DIM)
    v = f_v.reshape(B, S, FOX_HEADS, FOX_HEAD_DIM)
    log_f = jax.nn.log_sigmoid((f_logit + b_forget).astype(jnp.float32))
    o_b = forgetting_attention(q, k, v, log_f)
    y_b = o_b.reshape(B, S, FOX_W) * jax.nn.silu(f_gate)

    mn = rmsnorm(mem, g_mem)
    mk = (mn @ w_mem_k).reshape(Bm, M, MEM_HEADS, MEM_HEAD_DIM)
    mv = (mn @ w_mem_v).reshape(Bm, M, MEM_HEADS, MEM_HEAD_DIM)
    o_m = memory_attention(m_q.reshape(B, S, MEM_HEADS, MEM_HEAD_DIM), mk, mv)
    y_m = o_m.reshape(B, S, MEM_W) * jax.nn.silu(m_gate)

    ys = jnp.stack([y_a, y_b, y_m], axis=2)
    proj = jnp.einsum("bsnw,nwd->bsnd", ys, w_branch)
    gates = jax.nn.sigmoid(merge_logit.reshape(B, S, N_BRANCH, D) + b_merge)
    merged = jnp.sum(gates * proj, axis=2)
    out = merged @ w_out
    return x + rmsnorm(out, g_post)


def reference(x, mem, g_pre, w_in, conv_w, conv_b, w_lru_r, b_lru_r, w_lru_i, b_lru_i,
              lru_lambda, b_forget, g_mem, w_mem_k, w_mem_v, w_branch, b_merge, w_out, g_post):
    h = x
    for _ in range(DEPTH):
        h = hybrid_layer(h, mem, g_pre, w_in, conv_w, conv_b, w_lru_r, b_lru_r, w_lru_i, b_lru_i,
                         lru_lambda, b_forget, g_mem, w_mem_k, w_mem_v, w_branch, b_merge, w_out, g_post)
    return h
```

```python
import functools
import math

import numpy as np
import jax
import jax.numpy as jnp
from jax import lax
from jax.experimental import pallas as pl
from jax.experimental.pallas import tpu as pltpu

D = 1024
TILE = 512
FOX_HEADS = 16
FOX_DH = 64
MEM_HEADS = 4
MEM_DH = 256
LRU_GROUPS = 4
LRU_GW = D // LRU_GROUPS
LRU_C = 8.0
CONV_W = 4
RMS_EPS = 1e-6
NEG_INF = -1e30
LANES = 128
SUBLANES = 8
VMEM_LIMIT = 56 * 1024 * 1024

f32 = jnp.float32
bf16 = jnp.bfloat16


def _dot(a, b):
    return jnp.dot(a, b, preferred_element_type=f32)


def _dot_nt(a, b):
    return lax.dot_general(a, b, (((1,), (1,)), ((), ())), preferred_element_type=f32)


def _rms(x, g):
    return x * lax.rsqrt(jnp.mean(x * x, axis=-1, keepdims=True) + RMS_EPS) * g


def _sigmoid(z):
    return 1.0 / (1.0 + jnp.exp(-z))


def _softplus(z):
    return jnp.maximum(z, 0.0) + jnp.log1p(jnp.exp(-jnp.abs(z)))


def _split3(v):
    hi = v.astype(bf16)
    r1 = v - hi.astype(f32)
    mid = r1.astype(bf16)
    lo = (r1 - mid.astype(f32)).astype(bf16)
    return hi, mid, lo


def _mem_kv_kernel(mem_ref, g_ref, wk_ref, wv_ref, mk_ref, mv_ref):
    mn = _rms(mem_ref[0], g_ref[...]).astype(bf16)
    mk_ref[0] = _dot(mn, wk_ref[...]).astype(bf16)
    mv_ref[0] = _dot(mn, wv_ref[...]).astype(bf16)


def _mem_kv(mem, g_mem, wk, wv):
    B, M, _ = mem.shape
    full = lambda shape: pl.BlockSpec(shape, lambda b: (0,) * len(shape))
    return pl.pallas_call(
        _mem_kv_kernel,
        grid=(B,),
        in_specs=[pl.BlockSpec((1, M, D), lambda b: (b, 0, 0)), full((1, D)),
                  full((D, D)), full((D, D))],
        out_specs=[pl.BlockSpec((1, M, D), lambda b: (b, 0, 0))] * 2,
        out_shape=[jax.ShapeDtypeStruct((B, M, D), bf16)] * 2,
        compiler_params=pltpu.CompilerParams(
            dimension_semantics=("arbitrary",), vmem_limit_bytes=VMEM_LIMIT),
        name="mem_kv",
    )(mem, g_mem, wk, wv)


def _lru_kernel(x_ref, g_ref, wax_ref, wag_ref, wmg_ref, cw_ref, cb_ref, wr_ref, wi_ref,
                br_ref, bi_ref, lam_ref, wb_ref, bm_ref, pa_ref,
                xpad_ref, a_ref, u_ref, h_ref, hc_ref):
    T = x_ref.shape[1]
    t = pl.program_id(1)

    @pl.when(t == 0)
    def _():
        xpad_ref[0:SUBLANES, :] = jnp.zeros((SUBLANES, D), f32)
        hc_ref[...] = jnp.zeros((SUBLANES, D), f32)

    xn = _rms(x_ref[0], g_ref[...]).astype(bf16)

    ax = _dot(xn, wax_ref[...])
    xpad_ref[SUBLANES:SUBLANES + T, :] = ax
    xc = cb_ref[...] + cw_ref[0:1, :] * xpad_ref[SUBLANES - 3:SUBLANES - 3 + T, :]
    for k in range(1, CONV_W):
        off = SUBLANES - (CONV_W - 1) + k
        xc = xc + cw_ref[k:k + 1, :] * xpad_ref[off:off + T, :]
    xpad_ref[0:SUBLANES, :] = ax[T - SUBLANES:T, :]

    xcb = xc.astype(bf16)
    r = jnp.concatenate(
        [_dot(xcb[:, g * LRU_GW:(g + 1) * LRU_GW], wr_ref[g]) for g in range(LRU_GROUPS)], axis=1)
    i = jnp.concatenate(
        [_dot(xcb[:, g * LRU_GW:(g + 1) * LRU_GW], wi_ref[g]) for g in range(LRU_GROUPS)], axis=1)
    r = _sigmoid(r + br_ref[...])
    i = _sigmoid(i + bi_ref[...])
    log_a = (-LRU_C * r) * _softplus(-lam_ref[...])
    a = jnp.exp(log_a)
    u = jnp.sqrt(-jnp.tanh(log_a) * (a * a + 1.0)) * (i * xc)

    row = lax.broadcasted_iota(jnp.int32, (T, D), 0) % SUBLANES
    shift = 1
    while shift < SUBLANES:
        keep = row >= shift
        a_prev = jnp.where(keep, pltpu.roll(a, shift, 0), 1.0)
        u_prev = jnp.where(keep, pltpu.roll(u, shift, 0), 0.0)
        u = a * u_prev + u
        a = a * a_prev
        shift *= 2
    a_ref[...] = a
    u_ref[...] = u

    def group(c, carry):
        r0 = pl.multiple_of(c * SUBLANES, SUBLANES)
        hg = a_ref[pl.ds(r0, SUBLANES), :] * carry + u_ref[pl.ds(r0, SUBLANES), :]
        h_ref[pl.ds(r0, SUBLANES), :] = hg
        return jnp.broadcast_to(hg[SUBLANES - 1:SUBLANES, :], (SUBLANES, D))

    hc_ref[...] = lax.fori_loop(0, T // SUBLANES, group, hc_ref[...], unroll=8)

    ag = _dot(xn, wag_ref[...])
    y_a = (h_ref[...] * (ag * _sigmoid(ag))).astype(bf16)
    gate = _sigmoid(_dot(xn, wmg_ref[...]) + bm_ref[...])
    pa_ref[0] = (gate * _dot(y_a, wb_ref[...])).astype(bf16)


def _lru(x, g_pre, wax, wag, wmg, conv_w, conv_b, wr, wi, br, bi, lam, wb, bm):
    B, S, _ = x.shape
    T = min(TILE, S)
    full = lambda shape: pl.BlockSpec(shape, lambda b, t: (0,) * len(shape))
    tile = pl.BlockSpec((1, T, D), lambda b, t: (b, t, 0))
    return pl.pallas_call(
        _lru_kernel,
        grid=(B, S // T),
        in_specs=[tile, full((1, D)), full((D, D)), full((D, D)), full((D, D)),
                  full((CONV_W, D)), full((1, D)),
                  full((LRU_GROUPS, LRU_GW, LRU_GW)), full((LRU_GROUPS, LRU_GW, LRU_GW)),
                  full((1, D)), full((1, D)), full((1, D)), full((D, D)), full((1, D))],
        out_specs=tile,
        out_shape=jax.ShapeDtypeStruct((B, S, D), bf16),
        scratch_shapes=[pltpu.VMEM((T + SUBLANES, D), f32), pltpu.VMEM((T, D), f32),
                        pltpu.VMEM((T, D), f32), pltpu.VMEM((T, D), f32),
                        pltpu.VMEM((SUBLANES, D), f32)],
        compiler_params=pltpu.CompilerParams(
            dimension_semantics=("arbitrary", "arbitrary"), vmem_limit_bytes=VMEM_LIMIT),
        name="lru",
    )(x, g_pre, wax, wag, wmg, conv_w, conv_b, wr, wi, br, bi, lam, wb, bm)


def _memattn_kernel(x_ref, g_ref, wq_ref, wg_ref, wmg_ref, mk_ref, mv_ref, wb_ref, bm_ref,
                    pa_ref, out_ref):
    xn = _rms(x_ref[0], g_ref[...]).astype(bf16)
    q = (_dot(xn, wq_ref[...]) * (1.0 / math.sqrt(MEM_DH))).astype(bf16)
    outs = []
    for h in range(MEM_HEADS):
        sl = slice(h * MEM_DH, (h + 1) * MEM_DH)
        s = _dot_nt(q[:, sl], mk_ref[0, :, sl])
        p = jnp.exp(s - jnp.max(s, axis=-1, keepdims=True))
        l = jnp.sum(p, axis=-1, keepdims=True)
        outs.append(_dot(p.astype(bf16), mv_ref[0, :, sl]) / l)
    o = jnp.concatenate(outs, axis=1)
    mg = _dot(xn, wg_ref[...])
    y_m = (o * (mg * _sigmoid(mg))).astype(bf16)
    gate = _sigmoid(_dot(xn, wmg_ref[...]) + bm_ref[...])
    out_ref[0] = (pa_ref[0].astype(f32) + gate * _dot(y_m, wb_ref[...])).astype(bf16)


def _memattn(x, g_pre, wq, wg, wmg, mk, mv, wb, bm, pa):
    B, S, _ = x.shape
    M = mk.shape[1]
    T = min(TILE, S)
    full = lambda shape: pl.BlockSpec(shape, lambda b, t: (0,) * len(shape))
    tile = pl.BlockSpec((1, T, D), lambda b, t: (b, t, 0))
    memb = pl.BlockSpec((1, M, D), lambda b, t: (b, 0, 0))
    return pl.pallas_call(
        _memattn_kernel,
        grid=(B, S // T),
        in_specs=[tile, full((1, D)), full((D, D)), full((D, D)), full((D, D)), memb, memb,
                  full((D, D)), full((1, D)), tile],
        out_specs=tile,
        out_shape=jax.ShapeDtypeStruct((B, S, D), bf16),
        compiler_params=pltpu.CompilerParams(
            dimension_semantics=("arbitrary", "arbitrary"), vmem_limit_bytes=VMEM_LIMIT),
        name="memattn",
    )(x, g_pre, wq, wg, wmg, mk, mv, wb, bm, pa)


def _fox_proj_kernel(x_ref, g_ref, wqt_ref, wk_ref, wvt_ref, wf_ref, bf_ref, wfg_ref, wmg_ref,
                     bm_ref, tri_ref, place_ref,
                     qt_ref, k_ref, kb_ref, vt_ref, sg_ref, gb_ref, cref_ref, carry_ref):
    T = x_ref.shape[1]
    t = pl.program_id(1)

    @pl.when(t == 0)
    def _():
        carry_ref[...] = jnp.zeros((SUBLANES, LANES), f32)

    xn = _rms(x_ref[0], g_ref[...]).astype(bf16)
    qt_ref[0] = (_dot_nt(wqt_ref[...], xn) * (1.0 / math.sqrt(FOX_DH))).astype(bf16)
    vt_ref[0] = _dot_nt(wvt_ref[...], xn).astype(bf16)
    kz = _dot(xn, wk_ref[...]).astype(bf16)
    for j in range(D // LANES):
        k_ref[0, j] = kz[:, j * LANES:(j + 1) * LANES]

    lane = lax.broadcasted_iota(jnp.int32, (T, LANES), 1)
    z = _dot(xn, wf_ref[...]) + bf_ref[...]
    log_f = jnp.where(lane < FOX_HEADS, -_softplus(-z), 0.0)
    tri = tri_ref[...]
    d = sum(_dot(tri, part) for part in _split3(log_f))
    kb = sum(_dot(part, place_ref[n]) for n, part in enumerate(_split3(-d)))
    kb_ref[0] = kb.astype(bf16)

    carry = carry_ref[0:1, :]
    ident = (lax.broadcasted_iota(jnp.int32, (FOX_HEADS, LANES), 0)
             == lax.broadcasted_iota(jnp.int32, (FOX_HEADS, LANES), 1))
    col = jnp.sum(jnp.where(ident, jnp.broadcast_to(carry, (FOX_HEADS, LANES)), 0.0),
                  axis=1, keepdims=True)
    cref_ref[0, 0] = jnp.broadcast_to(col, (FOX_HEADS, T))
    carry_ref[...] = jnp.broadcast_to(carry + d[T - 1:T, :], (SUBLANES, LANES))

    fg = _dot(xn, wfg_ref[...])
    sg_ref[0] = (fg * _sigmoid(fg)).astype(bf16)
    gb_ref[0] = _sigmoid(_dot(xn, wmg_ref[...]) + bm_ref[...]).astype(bf16)


def _fox_proj(x, g_pre, wqt, wk, wvt, wf, bf, wfg, wmg, bm):
    B, S, _ = x.shape
    T = min(TILE, S)
    nt = S // T
    tri = jnp.asarray(np.tril(np.ones((T, T), np.float32)), bf16)
    place = np.zeros((3, LANES, LANES), np.float32)
    for n in range(3):
        for h in range(FOX_HEADS):
            place[n, h, n * FOX_HEADS + h] = 1.0
    place = jnp.asarray(place, bf16)
    full = lambda shape: pl.BlockSpec(shape, lambda b, t: (0,) * len(shape))
    tile = pl.BlockSpec((1, T, D), lambda b, t: (b, t, 0))
    tile_t = pl.BlockSpec((1, D, T), lambda b, t: (b, 0, t))
    return pl.pallas_call(
        _fox_proj_kernel,
        grid=(B, nt),
        in_specs=[tile, full((1, D)), full((D, D)), full((D, D)), full((D, D)),
                  full((D, LANES)), full((1, LANES)), full((D, D)), full((D, D)), full((1, D)),
                  full((T, T)), full((3, LANES, LANES))],
        out_specs=[tile_t,
                   pl.BlockSpec((1, D // LANES, T, LANES), lambda b, t: (b, 0, t, 0)),
                   pl.BlockSpec((1, T, LANES), lambda b, t: (b, t, 0)),
                   tile_t, tile, tile,
                   pl.BlockSpec((1, 1, FOX_HEADS, T), lambda b, t: (b, t, 0, 0))],
        out_shape=[jax.ShapeDtypeStruct((B, D, S), bf16),
                   jax.ShapeDtypeStruct((B, D // LANES, S, LANES), bf16),
                   jax.ShapeDtypeStruct((B, S, LANES), bf16),
                   jax.ShapeDtypeStruct((B, D, S), bf16),
                   jax.ShapeDtypeStruct((B, S, D), bf16),
                   jax.ShapeDtypeStruct((B, S, D), bf16),
                   jax.ShapeDtypeStruct((B, nt, FOX_HEADS, T), f32)],
        scratch_shapes=[pltpu.VMEM((SUBLANES, LANES), f32)],
        compiler_params=pltpu.CompilerParams(
            dimension_semantics=("arbitrary", "arbitrary"), vmem_limit_bytes=VMEM_LIMIT),
        name="fox_proj",
    )(x, g_pre, wqt, wk, wvt, wf, bf, wfg, wmg, bm, tri, place)


def _fox_attn_kernel(qi_ref, kj_ref, qt_ref, k_ref, kb_ref, vt_ref, cref_ref, sg_ref, gb_ref,
                     pam_ref, x_ref, wb_ref, wo_ref, gpost_ref, y_ref,
                     qa_ref, ot_ref, m_ref, l_ref):
    T = x_ref.shape[1]
    step = pl.program_id(1)
    qi = qi_ref[step]
    kj = kj_ref[step]
    pair_rows = 2 * FOX_DH

    @pl.when(kj == 0)
    def _():
        rows = lax.broadcasted_iota(jnp.int32, (pair_rows, T), 0)
        for h in range(FOX_HEADS):
            j, e = divmod(h, 2)
            qpair = qt_ref[0, j * pair_rows:(j + 1) * pair_rows, :].astype(f32)
            mine = (rows >= FOX_DH) if e else (rows < FOX_DH)
            qa_ref[h, 0:pair_rows, :] = jnp.where(mine, qpair, 0.0).astype(bf16)
            sel = (rows == h) | (rows == FOX_HEADS + h) | (rows == 2 * FOX_HEADS + h)
            qa_ref[h, pair_rows:2 * pair_rows, :] = jnp.where(sel, 1.0, 0.0).astype(bf16)
        ot_ref[...] = jnp.zeros(ot_ref.shape, f32)
        m_ref[...] = jnp.full(m_ref.shape, NEG_INF, f32)
        l_ref[...] = jnp.zeros(l_ref.shape, f32)

    def block(masked):
        kbias = kb_ref[0]
        if masked:
            causal = (lax.broadcasted_iota(jnp.int32, (T, T), 0)
                      <= lax.broadcasted_iota(jnp.int32, (T, T), 1))

        def pair(j, _):
            lhs = jnp.concatenate([k_ref[0, j], kbias], axis=1)
            for e in range(2):
                h = 2 * j + e
                st = _dot(lhs, qa_ref[h])
                if masked:
                    st = jnp.where(causal, st, NEG_INF)
                cr = cref_ref[0, 0, pl.ds(h, 1), :]
                m_old = m_ref[pl.ds(h, 1), :]
                m_new = jnp.maximum(m_old, jnp.max(st, axis=0, keepdims=True) - cr)
                alpha = jnp.exp(m_old - m_new)
                p = jnp.exp(st - (m_new + cr))
                l_ref[pl.ds(h, 1), :] = alpha * l_ref[pl.ds(h, 1), :] + jnp.sum(p, axis=0, keepdims=True)
                m_ref[pl.ds(h, 1), :] = m_new
                r0 = pl.multiple_of(h * FOX_DH, FOX_DH)
                pv = _dot(vt_ref[0, pl.ds(r0, FOX_DH), :], p.astype(bf16))
                ot_ref[pl.ds(r0, FOX_DH), :] = alpha * ot_ref[pl.ds(r0, FOX_DH), :] + pv
            return 0

        lax.fori_loop(0, FOX_HEADS // 2, pair, 0)

    @pl.when(kj < qi)
    def _():
        block(False)

    @pl.when(kj == qi)
    def _():
        block(True)
        for h in range(FOX_HEADS):
            sl = slice(h * FOX_DH, (h + 1) * FOX_DH)
            ot_ref[sl, :] = ot_ref[sl, :] / l_ref[h:h + 1, :]
        o = ot_ref[...].T
        y_b = (o * sg_ref[0].astype(f32)).astype(bf16)
        merged = pam_ref[0].astype(f32) + gb_ref[0].astype(f32) * _dot(y_b, wb_ref[...])
        out = _dot(merged.astype(bf16), wo_ref[...])
        y_ref[0] = x_ref[0] + _rms(out, gpost_ref[...])


def _fox_attn(qt, k, kb, vt, cref, sg, gb, pam, x, wb, wo, g_post):
    B, S, _ = x.shape
    T = min(TILE, S)
    nt = S // T
    pairs = [(q, j) for q in range(nt) for j in range(q + 1)]
    qi_tab = jnp.asarray([p[0] for p in pairs], jnp.int32)
    kj_tab = jnp.asarray([p[1] for p in pairs], jnp.int32)
    full = lambda shape: pl.BlockSpec(shape, lambda b, s, qi, kj: (0,) * len(shape))
    qtile = pl.BlockSpec((1, T, D), lambda b, s, qi, kj: (b, qi[s], 0))
    grid_spec = pltpu.PrefetchScalarGridSpec(
        num_scalar_prefetch=2,
        grid=(B, len(pairs)),
        in_specs=[pl.BlockSpec((1, D, T), lambda b, s, qi, kj: (b, 0, qi[s])),
                  pl.BlockSpec((1, D // LANES, T, LANES), lambda b, s, qi, kj: (b, 0, kj[s], 0)),
                  pl.BlockSpec((1, T, LANES), lambda b, s, qi, kj: (b, kj[s], 0)),
                  pl.BlockSpec((1, D, T), lambda b, s, qi, kj: (b, 0, kj[s])),
                  pl.BlockSpec((1, 1, FOX_HEADS, T), lambda b, s, qi, kj: (b, kj[s], 0, 0)),
                  qtile, qtile, qtile, qtile, full((D, D)), full((D, D)), full((1, D))],
        out_specs=qtile,
        scratch_shapes=[pltpu.VMEM((FOX_HEADS, 4 * FOX_DH, T), bf16), pltpu.VMEM((D, T), f32),
                        pltpu.VMEM((FOX_HEADS, T), f32), pltpu.VMEM((FOX_HEADS, T), f32)],
    )
    return pl.pallas_call(
        _fox_attn_kernel,
        grid_spec=grid_spec,
        out_shape=jax.ShapeDtypeStruct((B, S, D), f32),
        compiler_params=pltpu.CompilerParams(
            dimension_semantics=("arbitrary", "arbitrary"), vmem_limit_bytes=VMEM_LIMIT),
        name="fox_attn",
    )(qi_tab, kj_tab, qt, k, kb, vt, cref, sg, gb, pam, x, wb, wo, g_post)


def _block_diag_groups(w):
    nb, bw, _ = w.shape
    per = nb // LRU_GROUPS
    out = jnp.zeros((LRU_GROUPS, per * bw, per * bw), w.dtype)
    for g in range(LRU_GROUPS):
        for n in range(per):
            out = out.at[g, n * bw:(n + 1) * bw, n * bw:(n + 1) * bw].set(w[g * per + n])
    return out


def kernel(x, mem, g_pre, w_in, conv_w, conv_b, w_lru_r, b_lru_r, w_lru_i, b_lru_i, lru_lambda,
           b_forget, g_mem, w_mem_k, w_mem_v, w_branch, b_merge, w_out, g_post):
    row = lambda v: v.reshape(1, -1).astype(f32)
    sizes = (D, D, D, D, D, FOX_HEADS, D, D, D, 3 * D)
    offs = np.concatenate([[0], np.cumsum(sizes)])
    cols = [w_in[:, offs[n]:offs[n + 1]] for n in range(len(sizes))]
    w_ax, w_ag, w_fq, w_fk, w_fv, w_ff, w_fg, w_mq, w_mg, w_merge = cols
    w_mga, w_mgb, w_mgc = (w_merge[:, n * D:(n + 1) * D].astype(bf16) for n in range(3))
    w_ff = jnp.pad(w_ff, ((0, 0), (0, LANES - FOX_HEADS))).astype(bf16)
    b_f = jnp.pad(b_forget, (0, LANES - FOX_HEADS)).reshape(1, LANES).astype(f32)
    wb = w_branch.astype(bf16)
    g_pre_r = row(g_pre)

    mk, mv = _mem_kv(mem, row(g_mem), w_mem_k.astype(bf16), w_mem_v.astype(bf16))
    pa = _lru(x, g_pre_r, w_ax.astype(bf16), w_ag.astype(bf16), w_mga, conv_w.astype(f32),
              row(conv_b), _block_diag_groups(w_lru_r).astype(bf16),
              _block_diag_groups(w_lru_i).astype(bf16), row(b_lru_r), row(b_lru_i),
              row(lru_lambda), wb[0], row(b_merge[0]))
    pam = _memattn(x, g_pre_r, w_mq.astype(bf16), w_mg.astype(bf16), w_mgc, mk, mv, wb[2],
                   row(b_merge[2]), pa)
    qt, k, kb, vt, sg, gb, cref = _fox_proj(
        x, g_pre_r, w_fq.T.astype(bf16), w_fk.astype(bf16), w_fv.T.astype(bf16), w_ff, b_f,
        w_fg.astype(bf16), w_mgb, row(b_merge[1]))
    return _fox_attn(qt, k, kb, vt, cref, sg, gb, pam, x, wb[1], w_out.astype(bf16), row(g_post))
```

```python
import functools
import math

import numpy as np
import jax
import jax.numpy as jnp
from jax import lax
from jax.experimental import pallas as pl
from jax.experimental.pallas import tpu as pltpu

D = 1024
TILE = 512
FOX_HEADS = 16
FOX_DH = 64
MEM_HEADS = 4
MEM_DH = 256
LRU_GROUPS = 4
LRU_GW = D // LRU_GROUPS
LRU_C = 8.0
CONV_W = 4
RMS_EPS = 1e-6
NEG_INF = -1e30
LOG2E = math.log2(math.e)
V_ROWS = FOX_DH + 16
LANES = 128
SUBLANES = 8
VMEM_LIMIT = 56 * 1024 * 1024

f32 = jnp.float32
bf16 = jnp.bfloat16


def _dot(a, b):
    return jnp.dot(a, b, preferred_element_type=f32)


def _dot_nt(a, b):
    return lax.dot_general(a, b, (((1,), (1,)), ((), ())), preferred_element_type=f32)


def _rms(x, g):
    return x * lax.rsqrt(jnp.mean(x * x, axis=-1, keepdims=True) + RMS_EPS) * g


def _sigmoid(z):
    return 1.0 / (1.0 + jnp.exp(-z))


def _softplus(z):
    return jnp.maximum(z, 0.0) + jnp.log1p(jnp.exp(-jnp.abs(z)))


def _split3(v):
    hi = v.astype(bf16)
    r1 = v - hi.astype(f32)
    mid = r1.astype(bf16)
    lo = (r1 - mid.astype(f32)).astype(bf16)
    return hi, mid, lo


def _mem_kv_kernel(mem_ref, g_ref, wk_ref, wv_ref, mk_ref, mv_ref):
    mn = _rms(mem_ref[0], g_ref[...]).astype(bf16)
    mk_ref[0] = _dot(mn, wk_ref[...]).astype(bf16)
    mv_ref[0] = _dot(mn, wv_ref[...]).astype(bf16)


def _mem_kv(mem, g_mem, wk, wv):
    B, M, _ = mem.shape
    full = lambda shape: pl.BlockSpec(shape, lambda b: (0,) * len(shape))
    return pl.pallas_call(
        _mem_kv_kernel,
        grid=(B,),
        in_specs=[pl.BlockSpec((1, M, D), lambda b: (b, 0, 0)), full((1, D)),
                  full((D, D)), full((D, D))],
        out_specs=[pl.BlockSpec((1, M, D), lambda b: (b, 0, 0))] * 2,
        out_shape=[jax.ShapeDtypeStruct((B, M, D), bf16)] * 2,
        compiler_params=pltpu.CompilerParams(
            dimension_semantics=("arbitrary",), vmem_limit_bytes=VMEM_LIMIT),
        name="mem_kv",
    )(mem, g_mem, wk, wv)


def _lru_kernel(x_ref, g_ref, wax_ref, wag_ref, wmg_ref, cw_ref, cb_ref, wr_ref, wi_ref,
                br_ref, bi_ref, lam_ref, wb_ref, bm_ref, pa_ref,
                xpad_ref, a_ref, u_ref, h_ref, hc_ref):
    T = x_ref.shape[1]
    t = pl.program_id(1)

    @pl.when(t == 0)
    def _():
        xpad_ref[0:SUBLANES, :] = jnp.zeros((SUBLANES, D), f32)
        hc_ref[...] = jnp.zeros((SUBLANES, D), f32)

    xn = _rms(x_ref[0], g_ref[...]).astype(bf16)

    ax = _dot(xn, wax_ref[...])
    xpad_ref[SUBLANES:SUBLANES + T, :] = ax
    xc = cb_ref[...] + cw_ref[0:1, :] * xpad_ref[SUBLANES - 3:SUBLANES - 3 + T, :]
    for k in range(1, CONV_W):
        off = SUBLANES - (CONV_W - 1) + k
        xc = xc + cw_ref[k:k + 1, :] * xpad_ref[off:off + T, :]
    xpad_ref[0:SUBLANES, :] = ax[T - SUBLANES:T, :]

    xcb = xc.astype(bf16)
    r = jnp.concatenate(
        [_dot(xcb[:, g * LRU_GW:(g + 1) * LRU_GW], wr_ref[g]) for g in range(LRU_GROUPS)], axis=1)
    i = jnp.concatenate(
        [_dot(xcb[:, g * LRU_GW:(g + 1) * LRU_GW], wi_ref[g]) for g in range(LRU_GROUPS)], axis=1)
    r = _sigmoid(r + br_ref[...])
    i = _sigmoid(i + bi_ref[...])
    log_a = (-LRU_C * r) * _softplus(-lam_ref[...])
    a = jnp.exp(log_a)
    u = jnp.sqrt(-jnp.tanh(log_a) * (a * a + 1.0)) * (i * xc)

    row = lax.broadcasted_iota(jnp.int32, (T, D), 0) % SUBLANES
    shift = 1
    while shift < SUBLANES:
        keep = row >= shift
        a_prev = jnp.where(keep, pltpu.roll(a, shift, 0), 1.0)
        u_prev = jnp.where(keep, pltpu.roll(u, shift, 0), 0.0)
        u = a * u_prev + u
        a = a * a_prev
        shift *= 2
    a_ref[...] = a
    u_ref[...] = u

    def group(c, carry):
        r0 = pl.multiple_of(c * SUBLANES, SUBLANES)
        hg = a_ref[pl.ds(r0, SUBLANES), :] * carry + u_ref[pl.ds(r0, SUBLANES), :]
        h_ref[pl.ds(r0, SUBLANES), :] = hg
        return jnp.broadcast_to(hg[SUBLANES - 1:SUBLANES, :], (SUBLANES, D))

    hc_ref[...] = lax.fori_loop(0, T // SUBLANES, group, hc_ref[...], unroll=8)

    ag = _dot(xn, wag_ref[...])
    y_a = (h_ref[...] * (ag * _sigmoid(ag))).astype(bf16)
    gate = _sigmoid(_dot(xn, wmg_ref[...]) + bm_ref[...])
    pa_ref[0] = (gate * _dot(y_a, wb_ref[...])).astype(bf16)


def _lru(x, g_pre, wax, wag, wmg, conv_w, conv_b, wr, wi, br, bi, lam, wb, bm):
    B, S, _ = x.shape
    T = min(TILE, S)
    full = lambda shape: pl.BlockSpec(shape, lambda b, t: (0,) * len(shape))
    tile = pl.BlockSpec((1, T, D), lambda b, t: (b, t, 0))
    return pl.pallas_call(
        _lru_kernel,
        grid=(B, S // T),
        in_specs=[tile, full((1, D)), full((D, D)), full((D, D)), full((D, D)),
                  full((CONV_W, D)), full((1, D)),
                  full((LRU_GROUPS, LRU_GW, LRU_GW)), full((LRU_GROUPS, LRU_GW, LRU_GW)),
                  full((1, D)), full((1, D)), full((1, D)), full((D, D)), full((1, D))],
        out_specs=tile,
        out_shape=jax.ShapeDtypeStruct((B, S, D), bf16),
        scratch_shapes=[pltpu.VMEM((T + SUBLANES, D), f32), pltpu.VMEM((T, D), f32),
                        pltpu.VMEM((T, D), f32), pltpu.VMEM((T, D), f32),
                        pltpu.VMEM((SUBLANES, D), f32)],
        compiler_params=pltpu.CompilerParams(
            dimension_semantics=("arbitrary", "arbitrary"), vmem_limit_bytes=VMEM_LIMIT),
        name="lru",
    )(x, g_pre, wax, wag, wmg, conv_w, conv_b, wr, wi, br, bi, lam, wb, bm)


def _memattn_kernel(x_ref, g_ref, wq_ref, wg_ref, wmg_ref, mk_ref, mv_ref, wb_ref, bm_ref,
                    pa_ref, out_ref):
    xn = _rms(x_ref[0], g_ref[...]).astype(bf16)
    q = (_dot(xn, wq_ref[...]) * (1.0 / math.sqrt(MEM_DH))).astype(bf16)
    outs = []
    for h in range(MEM_HEADS):
        sl = slice(h * MEM_DH, (h + 1) * MEM_DH)
        s = _dot_nt(q[:, sl], mk_ref[0, :, sl])
        p = jnp.exp(s - jnp.max(s, axis=-1, keepdims=True))
        l = jnp.sum(p, axis=-1, keepdims=True)
        outs.append(_dot(p.astype(bf16), mv_ref[0, :, sl]) / l)
    o = jnp.concatenate(outs, axis=1)
    mg = _dot(xn, wg_ref[...])
    y_m = (o * (mg * _sigmoid(mg))).astype(bf16)
    gate = _sigmoid(_dot(xn, wmg_ref[...]) + bm_ref[...])
    out_ref[0] = (pa_ref[0].astype(f32) + gate * _dot(y_m, wb_ref[...])).astype(bf16)


def _memattn(x, g_pre, wq, wg, wmg, mk, mv, wb, bm, pa):
    B, S, _ = x.shape
    M = mk.shape[1]
    T = min(TILE, S)
    full = lambda shape: pl.BlockSpec(shape, lambda b, t: (0,) * len(shape))
    tile = pl.BlockSpec((1, T, D), lambda b, t: (b, t, 0))
    memb = pl.BlockSpec((1, M, D), lambda b, t: (b, 0, 0))
    return pl.pallas_call(
        _memattn_kernel,
        grid=(B, S // T),
        in_specs=[tile, full((1, D)), full((D, D)), full((D, D)), full((D, D)), memb, memb,
                  full((D, D)), full((1, D)), tile],
        out_specs=tile,
        out_shape=jax.ShapeDtypeStruct((B, S, D), bf16),
        compiler_params=pltpu.CompilerParams(
            dimension_semantics=("arbitrary", "arbitrary"), vmem_limit_bytes=VMEM_LIMIT),
        name="memattn",
    )(x, g_pre, wq, wg, wmg, mk, mv, wb, bm, pa)


def _fox_proj_kernel(x_ref, g_ref, wqt_ref, wk_ref, wvt_ref, wf_ref, bf_ref, wfg_ref, wmg_ref,
                     bm_ref, tri_ref, place_ref,
                     qt_ref, k_ref, kb_ref, vt_ref, sg_ref, gb_ref, cref_ref, carry_ref):
    T = x_ref.shape[1]
    t = pl.program_id(1)

    @pl.when(t == 0)
    def _():
        carry_ref[...] = jnp.zeros((SUBLANES, LANES), f32)

    xn = _rms(x_ref[0], g_ref[...]).astype(bf16)
    qt_ref[0] = (_dot_nt(wqt_ref[...], xn) * (LOG2E / math.sqrt(FOX_DH))).astype(bf16)
    vt = _dot_nt(wvt_ref[...], xn).astype(bf16)
    ones = jnp.ones((V_ROWS - FOX_DH, T), bf16)
    for h in range(FOX_HEADS):
        vt_ref[0, h * V_ROWS:h * V_ROWS + FOX_DH, :] = vt[h * FOX_DH:(h + 1) * FOX_DH, :]
        vt_ref[0, h * V_ROWS + FOX_DH:(h + 1) * V_ROWS, :] = ones
    kz = _dot(xn, wk_ref[...]).astype(bf16)
    for j in range(D // LANES):
        k_ref[0, j] = kz[:, j * LANES:(j + 1) * LANES]

    lane = lax.broadcasted_iota(jnp.int32, (T, LANES), 1)
    z = _dot(xn, wf_ref[...]) + bf_ref[...]
    log_f = jnp.where(lane < FOX_HEADS, -LOG2E * _softplus(-z), 0.0)
    tri = tri_ref[...]
    d = sum(_dot(tri, part) for part in _split3(log_f))
    kb = sum(_dot(part, place_ref[n]) for n, part in enumerate(_split3(-d)))
    kb_ref[0] = kb.astype(bf16)

    carry = carry_ref[0:1, :]
    ident = (lax.broadcasted_iota(jnp.int32, (FOX_HEADS, LANES), 0)
             == lax.broadcasted_iota(jnp.int32, (FOX_HEADS, LANES), 1))
    col = jnp.sum(jnp.where(ident, jnp.broadcast_to(carry, (FOX_HEADS, LANES)), 0.0),
                  axis=1, keepdims=True)
    cref_ref[0, 0] = jnp.broadcast_to(col, (FOX_HEADS, T))
    carry_ref[...] = jnp.broadcast_to(carry + d[T - 1:T, :], (SUBLANES, LANES))

    fg = _dot(xn, wfg_ref[...])
    sg_ref[0] = (fg * _sigmoid(fg)).astype(bf16)
    gb_ref[0] = _sigmoid(_dot(xn, wmg_ref[...]) + bm_ref[...]).astype(bf16)


def _fox_proj(x, g_pre, wqt, wk, wvt, wf, bf, wfg, wmg, bm):
    B, S, _ = x.shape
    T = min(TILE, S)
    nt = S // T
    tri = jnp.asarray(np.tril(np.ones((T, T), np.float32)), bf16)
    place = np.zeros((3, LANES, LANES), np.float32)
    for n in range(3):
        for h in range(FOX_HEADS):
            place[n, h, n * FOX_HEADS + h] = 1.0
    place = jnp.asarray(place, bf16)
    full = lambda shape: pl.BlockSpec(shape, lambda b, t: (0,) * len(shape))
    tile = pl.BlockSpec((1, T, D), lambda b, t: (b, t, 0))
    tile_t = pl.BlockSpec((1, D, T), lambda b, t: (b, 0, t))
    vrows = FOX_HEADS * V_ROWS
    return pl.pallas_call(
        _fox_proj_kernel,
        grid=(B, nt),
        in_specs=[tile, full((1, D)), full((D, D)), full((D, D)), full((D, D)),
                  full((D, LANES)), full((1, LANES)), full((D, D)), full((D, D)), full((1, D)),
                  full((T, T)), full((3, LANES, LANES))],
        out_specs=[tile_t,
                   pl.BlockSpec((1, D // LANES, T, LANES), lambda b, t: (b, 0, t, 0)),
                   pl.BlockSpec((1, T, LANES), lambda b, t: (b, t, 0)),
                   pl.BlockSpec((1, vrows, T), lambda b, t: (b, 0, t)), tile, tile,
                   pl.BlockSpec((1, 1, FOX_HEADS, T), lambda b, t: (b, t, 0, 0))],
        out_shape=[jax.ShapeDtypeStruct((B, D, S), bf16),
                   jax.ShapeDtypeStruct((B, D // LANES, S, LANES), bf16),
                   jax.ShapeDtypeStruct((B, S, LANES), bf16),
                   jax.ShapeDtypeStruct((B, vrows, S), bf16),
                   jax.ShapeDtypeStruct((B, S, D), bf16),
                   jax.ShapeDtypeStruct((B, S, D), bf16),
                   jax.ShapeDtypeStruct((B, nt, FOX_HEADS, T), f32)],
        scratch_shapes=[pltpu.VMEM((SUBLANES, LANES), f32)],
        compiler_params=pltpu.CompilerParams(
            dimension_semantics=("arbitrary", "arbitrary"), vmem_limit_bytes=VMEM_LIMIT),
        name="fox_proj",
    )(x, g_pre, wqt, wk, wvt, wf, bf, wfg, wmg, bm, tri, place)


def _fox_attn_kernel(qi_ref, kj_ref, qt_ref, k_ref, kb_ref, vt_ref, cref_ref, sg_ref, gb_ref,
                     pam_ref, x_ref, wb_ref, wo_ref, gpost_ref, y_ref,
                     qa_ref, acc_ref, m_ref, s_ref, o_ref):
    T = x_ref.shape[1]
    step = pl.program_id(1)
    qi = qi_ref[step]
    kj = kj_ref[step]
    pair_rows = 2 * FOX_DH

    @pl.when(kj == 0)
    def _():
        rows = lax.broadcasted_iota(jnp.int32, (pair_rows, T), 0)
        for h in range(FOX_HEADS):
            j, e = divmod(h, 2)
            qpair = qt_ref[0, j * pair_rows:(j + 1) * pair_rows, :].astype(f32)
            mine = (rows >= FOX_DH) if e else (rows < FOX_DH)
            qa_ref[h, 0:pair_rows, :] = jnp.where(mine, qpair, 0.0).astype(bf16)
            sel = (rows == h) | (rows == FOX_HEADS + h) | (rows == 2 * FOX_HEADS + h)
            qa_ref[h, pair_rows:2 * pair_rows, :] = jnp.where(sel, 1.0, 0.0).astype(bf16)
        acc_ref[...] = jnp.zeros(acc_ref.shape, f32)
        m_ref[...] = jnp.full(m_ref.shape, NEG_INF, f32)

    def block(masked):
        kbias = kb_ref[0]
        if masked:
            causal = (lax.broadcasted_iota(jnp.int32, (T, T), 0)
                      <= lax.broadcasted_iota(jnp.int32, (T, T), 1))

        def scores(h, j, slot):
            lhs = jnp.concatenate([k_ref[0, j], kbias], axis=1)
            st = _dot(lhs, qa_ref[h])
            if masked:
                st = jnp.where(causal, st, NEG_INF)
            s_ref[slot] = st
            return jnp.max(st, axis=0, keepdims=True)

        def update(h, slot, bmax):
            cr = cref_ref[0, 0, h:h + 1, :]
            m_old = m_ref[h:h + 1, :]
            m_new = jnp.maximum(m_old, bmax - cr)
            alpha = jnp.exp2(m_old - m_new)
            p = jnp.exp2(s_ref[slot] - (m_new + cr)).astype(bf16)
            m_ref[h:h + 1, :] = m_new
            rows = slice(h * V_ROWS, (h + 1) * V_ROWS)
            pv = _dot(vt_ref[0, rows, :], p)
            acc_ref[rows, :] = alpha * acc_ref[rows, :] + pv

        bmax = scores(0, 0, 0)
        for h in range(FOX_HEADS):
            if h + 1 < FOX_HEADS:
                bmax_next = scores(h + 1, (h + 1) // 2, (h + 1) % 2)
            update(h, h % 2, bmax)
            bmax = bmax_next

    @pl.when(kj < qi)
    def _():
        block(False)

    @pl.when(kj == qi)
    def _():
        block(True)
        for h in range(FOX_HEADS):
            r0 = h * V_ROWS
            o_ref[h * FOX_DH:(h + 1) * FOX_DH, :] = (
                acc_ref[r0:r0 + FOX_DH, :] / acc_ref[r0 + FOX_DH:r0 + FOX_DH + 1, :])
        o = o_ref[...].T
        y_b = (o * sg_ref[0].astype(f32)).astype(bf16)
        merged = pam_ref[0].astype(f32) + gb_ref[0].astype(f32) * _dot(y_b, wb_ref[...])
        out = _dot(merged.astype(bf16), wo_ref[...])
        y_ref[0] = x_ref[0] + _rms(out, gpost_ref[...])


def _fox_attn(qt, k, kb, vt, cref, sg, gb, pam, x, wb, wo, g_post):
    B, S, _ = x.shape
    T = min(TILE, S)
    nt = S // T
    pairs = [(q, j) for q in range(nt) for j in range(q + 1)]
    qi_tab = jnp.asarray([p[0] for p in pairs], jnp.int32)
    kj_tab = jnp.asarray([p[1] for p in pairs], jnp.int32)
    full = lambda shape: pl.BlockSpec(shape, lambda b, s, qi, kj: (0,) * len(shape))
    qtile = pl.BlockSpec((1, T, D), lambda b, s, qi, kj: (b, qi[s], 0))
    grid_spec = pltpu.PrefetchScalarGridSpec(
        num_scalar_prefetch=2,
        grid=(B, len(pairs)),
        in_specs=[pl.BlockSpec((1, D, T), lambda b, s, qi, kj: (b, 0, qi[s])),
                  pl.BlockSpec((1, D // LANES, T, LANES), lambda b, s, qi, kj: (b, 0, kj[s], 0)),
                  pl.BlockSpec((1, T, LANES), lambda b, s, qi, kj: (b, kj[s], 0)),
                  pl.BlockSpec((1, FOX_HEADS * V_ROWS, T), lambda b, s, qi, kj: (b, 0, kj[s])),
                  pl.BlockSpec((1, 1, FOX_HEADS, T), lambda b, s, qi, kj: (b, kj[s], 0, 0)),
                  qtile, qtile, qtile, qtile, full((D, D)), full((D, D)), full((1, D))],
        out_specs=qtile,
        scratch_shapes=[pltpu.VMEM((FOX_HEADS, 4 * FOX_DH, T), bf16),
                        pltpu.VMEM((FOX_HEADS * V_ROWS, T), f32),
                        pltpu.VMEM((FOX_HEADS, T), f32),
                        pltpu.VMEM((2, T, T), f32),
                        pltpu.VMEM((D, T), f32)],
    )
    return pl.pallas_call(
        _fox_attn_kernel,
        grid_spec=grid_spec,
        out_shape=jax.ShapeDtypeStruct((B, S, D), f32),
        compiler_params=pltpu.CompilerParams(
            dimension_semantics=("arbitrary", "arbitrary"), vmem_limit_bytes=VMEM_LIMIT),
        name="fox_attn",
    )(qi_tab, kj_tab, qt, k, kb, vt, cref, sg, gb, pam, x, wb, wo, g_post)


def _block_diag_groups(w):
    nb, bw, _ = w.shape
    per = nb // LRU_GROUPS
    out = jnp.zeros((LRU_GROUPS, per * bw, per * bw), w.dtype)
    for g in range(LRU_GROUPS):
        for n in range(per):
            out = out.at[g, n * bw:(n + 1) * bw, n * bw:(n + 1) * bw].set(w[g * per + n])
    return out


def kernel(x, mem, g_pre, w_in, conv_w, conv_b, w_lru_r, b_lru_r, w_lru_i, b_lru_i, lru_lambda,
           b_forget, g_mem, w_mem_k, w_mem_v, w_branch, b_merge, w_out, g_post):
    row = lambda v: v.reshape(1, -1).astype(f32)
    sizes = (D, D, D, D, D, FOX_HEADS, D, D, D, 3 * D)
    offs = np.concatenate([[0], np.cumsum(sizes)])
    cols = [w_in[:, offs[n]:offs[n + 1]] for n in range(len(sizes))]
    w_ax, w_ag, w_fq, w_fk, w_fv, w_ff, w_fg, w_mq, w_mg, w_merge = cols
    w_mga, w_mgb, w_mgc = (w_merge[:, n * D:(n + 1) * D].astype(bf16) for n in range(3))
    w_ff = jnp.pad(w_ff, ((0, 0), (0, LANES - FOX_HEADS))).astype(bf16)
    b_f = jnp.pad(b_forget, (0, LANES - FOX_HEADS)).reshape(1, LANES).astype(f32)
    wb = w_branch.astype(bf16)
    g_pre_r = row(g_pre)

    mk, mv = _mem_kv(mem, row(g_mem), w_mem_k.astype(bf16), w_mem_v.astype(bf16))
    pa = _lru(x, g_pre_r, w_ax.astype(bf16), w_ag.astype(bf16), w_mga, conv_w.astype(f32),
              row(conv_b), _block_diag_groups(w_lru_r).astype(bf16),
              _block_diag_groups(w_lru_i).astype(bf16), row(b_lru_r), row(b_lru_i),
              row(lru_lambda), wb[0], row(b_merge[0]))
    pam = _memattn(x, g_pre_r, w_mq.astype(bf16), w_mg.astype(bf16), w_mgc, mk, mv, wb[2],
                   row(b_merge[2]), pa)
    qt, k, kb, vt, sg, gb, cref = _fox_proj(
        x, g_pre_r, w_fq.T.astype(bf16), w_fk.astype(bf16), w_fv.T.astype(bf16), w_ff, b_f,
        w_fg.astype(bf16), w_mgb, row(b_merge[1]))
    return _fox_attn(qt, k, kb, vt, cref, sg, gb, pam, x, wb[1], w_out.astype(bf16), row(g_post))
```

```python
import functools
import math

import numpy as np
import jax
import jax.numpy as jnp
from jax import lax
from jax.experimental import pallas as pl
from jax.experimental.pallas import tpu as pltpu

D = 1024
TILE = 512
FOX_HEADS = 16
FOX_DH = 64
MEM_HEADS = 4
MEM_DH = 256
LRU_GROUPS = 4
LRU_GW = D // LRU_GROUPS
LRU_C = 8.0
LRU_CHUNKS = 2
CONV_W = 4
RMS_EPS = 1e-6
NEG_INF = -1e30
LOG2E = math.log2(math.e)
V_ROWS = FOX_DH + 16
LANES = 128
SUBLANES = 8
VMEM_LIMIT = 56 * 1024 * 1024

f32 = jnp.float32
bf16 = jnp.bfloat16


def _dot(a, b):
    return jnp.dot(a, b, preferred_element_type=f32)


def _dot_nt(a, b):
    return lax.dot_general(a, b, (((1,), (1,)), ((), ())), preferred_element_type=f32)


def _rms(x, g):
    return x * lax.rsqrt(jnp.mean(x * x, axis=-1, keepdims=True) + RMS_EPS) * g


def _sigmoid(z):
    return 1.0 / (1.0 + jnp.exp2(z * -LOG2E))


def _softplus(z):
    return jnp.maximum(z, 0.0) + jnp.log1p(jnp.exp(-jnp.abs(z)))


def _split3(v):
    hi = v.astype(bf16)
    r1 = v - hi.astype(f32)
    mid = r1.astype(bf16)
    lo = (r1 - mid.astype(f32)).astype(bf16)
    return hi, mid, lo


def _mem_kv_kernel(mem_ref, g_ref, wk_ref, wv_ref, mk_ref, mv_ref):
    mn = _rms(mem_ref[0], g_ref[...]).astype(bf16)
    mk_ref[0] = _dot(mn, wk_ref[...]).astype(bf16)
    mv_ref[0] = _dot(mn, wv_ref[...]).astype(bf16)


def _mem_kv(mem, g_mem, wk, wv):
    B, M, _ = mem.shape
    full = lambda shape: pl.BlockSpec(shape, lambda b: (0,) * len(shape))
    return pl.pallas_call(
        _mem_kv_kernel,
        grid=(B,),
        in_specs=[pl.BlockSpec((1, M, D), lambda b: (b, 0, 0)), full((1, D)),
                  full((D, D)), full((D, D))],
        out_specs=[pl.BlockSpec((1, M, D), lambda b: (b, 0, 0))] * 2,
        out_shape=[jax.ShapeDtypeStruct((B, M, D), bf16)] * 2,
        compiler_params=pltpu.CompilerParams(
            dimension_semantics=("arbitrary",), vmem_limit_bytes=VMEM_LIMIT),
        name="mem_kv",
    )(mem, g_mem, wk, wv)


def _lru_kernel(x_ref, g_ref, wax_ref, wag_ref, wmg_ref, cw_ref, cb_ref, wr_ref, wi_ref,
                br_ref, bi_ref, lam_ref, wb_ref, bm_ref, pa_ref,
                xn_ref, xpad_ref, h_ref, hc_ref):
    NB, _, TT, _ = x_ref.shape
    R = NB * TT
    halo = (CONV_W - 1) * NB
    t = pl.program_id(1)

    @pl.when(t == 0)
    def _():
        xpad_ref[0:halo, :] = jnp.zeros((halo, D), f32)
        hc_ref[...] = jnp.zeros((NB, D), f32)

    x = jnp.swapaxes(x_ref[:, 0], 0, 1).reshape(R, D)
    xn_ref[...] = _rms(x, g_ref[...]).astype(bf16)

    RC = R // LRU_CHUNKS
    TC = TT // LRU_CHUNKS
    softplus_lam = _softplus(-lam_ref[...])

    def rows_of(c):
        return slice(c * RC, (c + 1) * RC)

    def conv_in(c):
        ax = _dot(xn_ref[rows_of(c), :], wax_ref[...])
        xpad_ref[halo + c * RC:halo + (c + 1) * RC, :] = ax

    def gates(c):
        xc = cb_ref[...] + cw_ref[0:1, :] * xpad_ref[c * RC:(c + 1) * RC, :]
        for k in range(1, CONV_W):
            xc = xc + cw_ref[k:k + 1, :] * xpad_ref[c * RC + k * NB:(c + 1) * RC + k * NB, :]
        xcb = xc.astype(bf16)
        r = jnp.concatenate(
            [_dot(xcb[:, g * LRU_GW:(g + 1) * LRU_GW], wr_ref[g]) for g in range(LRU_GROUPS)],
            axis=1)
        i = jnp.concatenate(
            [_dot(xcb[:, g * LRU_GW:(g + 1) * LRU_GW], wi_ref[g]) for g in range(LRU_GROUPS)],
            axis=1)
        return xc, r, i

    def recurrence(c, xc, r, i, h):
        r = _sigmoid(r + br_ref[...])
        i = _sigmoid(i + bi_ref[...])
        log_a = (-LRU_C * r) * softplus_lam
        a = jnp.exp(log_a)
        z = -jnp.tanh(log_a) * (a * a + 1.0)
        u = jnp.where(z > 0.0, z * lax.rsqrt(z), 0.0) * (i * xc)
        for s in range(TC):
            rows = slice(s * NB, (s + 1) * NB)
            h = a[rows, :] * h + u[rows, :]
            h_ref[c * RC + s * NB:c * RC + (s + 1) * NB, :] = h
        return h

    def project(c, ag, mg):
        y_a = (h_ref[rows_of(c), :] * (ag * _sigmoid(ag))).astype(bf16)
        pa = _sigmoid(mg + bm_ref[...]) * _dot(y_a, wb_ref[...])
        pa_ref[:, 0, c * TC:(c + 1) * TC, :] = (
            jnp.swapaxes(pa.reshape(TC, NB, D), 0, 1).astype(bf16))

    chunks = range(LRU_CHUNKS)
    for c in chunks:
        conv_in(c)
    staged = []
    for c in chunks:
        ag = _dot(xn_ref[rows_of(c), :], wag_ref[...])
        staged.append((ag,) + gates(c))
    xpad_ref[0:halo, :] = xpad_ref[R:R + halo, :]
    mgs = [_dot(xn_ref[rows_of(c), :], wmg_ref[...]) for c in chunks]
    h = hc_ref[...]
    for c in chunks:
        ag, xc, r, i = staged[c]
        h = recurrence(c, xc, r, i, h)
        project(c, ag, mgs[c])
    hc_ref[...] = h


def _lru(x, g_pre, wax, wag, wmg, conv_w, conv_b, wr, wi, br, bi, lam, wb, bm):
    B, S, _ = x.shape
    NB = SUBLANES
    assert B % NB == 0
    TT = min(TILE // NB, S)
    full = lambda shape: pl.BlockSpec(shape, lambda g, t: (0,) * len(shape))
    tile = pl.BlockSpec((NB, 1, TT, D), lambda g, t: (g, t, 0, 0))
    pa = pl.pallas_call(
        _lru_kernel,
        grid=(B // NB, S // TT),
        in_specs=[tile, full((1, D)), full((D, D)), full((D, D)), full((D, D)),
                  full((CONV_W, D)), full((1, D)),
                  full((LRU_GROUPS, LRU_GW, LRU_GW)), full((LRU_GROUPS, LRU_GW, LRU_GW)),
                  full((1, D)), full((1, D)), full((1, D)), full((D, D)), full((1, D))],
        out_specs=tile,
        out_shape=jax.ShapeDtypeStruct((B, S // TT, TT, D), bf16),
        scratch_shapes=[pltpu.VMEM((TT * NB, D), bf16),
                        pltpu.VMEM(((TT + CONV_W - 1) * NB, D), f32), pltpu.VMEM((TT * NB, D), f32),
                        pltpu.VMEM((NB, D), f32)],
        compiler_params=pltpu.CompilerParams(
            dimension_semantics=("arbitrary", "arbitrary"), vmem_limit_bytes=VMEM_LIMIT),
        name="lru",
    )(x.reshape(B, S // TT, TT, D), g_pre, wax, wag, wmg, conv_w, conv_b, wr, wi, br, bi, lam,
      wb, bm)
    return pa.reshape(B, S, D)


def _memattn_kernel(x_ref, g_ref, wq_ref, wg_ref, wmg_ref, mk_ref, mv_ref, wb_ref, bm_ref,
                    pa_ref, out_ref):
    xn = _rms(x_ref[0], g_ref[...]).astype(bf16)
    q = (_dot(xn, wq_ref[...]) * (1.0 / math.sqrt(MEM_DH))).astype(bf16)
    outs = []
    for h in range(MEM_HEADS):
        sl = slice(h * MEM_DH, (h + 1) * MEM_DH)
        s = _dot_nt(q[:, sl], mk_ref[0, :, sl])
        p = jnp.exp(s - jnp.max(s, axis=-1, keepdims=True))
        l = jnp.sum(p, axis=-1, keepdims=True)
        outs.append(_dot(p.astype(bf16), mv_ref[0, :, sl]) / l)
    o = jnp.concatenate(outs, axis=1)
    mg = _dot(xn, wg_ref[...])
    y_m = (o * (mg * _sigmoid(mg))).astype(bf16)
    gate = _sigmoid(_dot(xn, wmg_ref[...]) + bm_ref[...])
    out_ref[0] = (pa_ref[0].astype(f32) + gate * _dot(y_m, wb_ref[...])).astype(bf16)


def _memattn(x, g_pre, wq, wg, wmg, mk, mv, wb, bm, pa):
    B, S, _ = x.shape
    M = mk.shape[1]
    T = min(TILE, S)
    full = lambda shape: pl.BlockSpec(shape, lambda b, t: (0,) * len(shape))
    tile = pl.BlockSpec((1, T, D), lambda b, t: (b, t, 0))
    memb = pl.BlockSpec((1, M, D), lambda b, t: (b, 0, 0))
    return pl.pallas_call(
        _memattn_kernel,
        grid=(B, S // T),
        in_specs=[tile, full((1, D)), full((D, D)), full((D, D)), full((D, D)), memb, memb,
                  full((D, D)), full((1, D)), tile],
        out_specs=tile,
        out_shape=jax.ShapeDtypeStruct((B, S, D), bf16),
        compiler_params=pltpu.CompilerParams(
            dimension_semantics=("arbitrary", "arbitrary"), vmem_limit_bytes=VMEM_LIMIT),
        name="memattn",
    )(x, g_pre, wq, wg, wmg, mk, mv, wb, bm, pa)


def _fox_proj_kernel(x_ref, g_ref, wqt_ref, wk_ref, wvt_ref, wf_ref, bf_ref, wfg_ref, wmg_ref,
                     bm_ref, tri_ref, place_ref,
                     qt_ref, k_ref, kb_ref, vt_ref, sg_ref, gb_ref, cref_ref, carry_ref):
    T = x_ref.shape[1]
    t = pl.program_id(1)

    @pl.when(t == 0)
    def _():
        carry_ref[...] = jnp.zeros((SUBLANES, LANES), f32)

    xn = _rms(x_ref[0], g_ref[...]).astype(bf16)
    qt_ref[0] = (_dot_nt(wqt_ref[...], xn) * (LOG2E / math.sqrt(FOX_DH))).astype(bf16)
    vt = _dot_nt(wvt_ref[...], xn).astype(bf16)
    ones = jnp.ones((V_ROWS - FOX_DH, T), bf16)
    for h in range(FOX_HEADS):
        vt_ref[0, h * V_ROWS:h * V_ROWS + FOX_DH, :] = vt[h * FOX_DH:(h + 1) * FOX_DH, :]
        vt_ref[0, h * V_ROWS + FOX_DH:(h + 1) * V_ROWS, :] = ones
    kz = _dot(xn, wk_ref[...]).astype(bf16)
    for j in range(D // LANES):
        k_ref[0, j] = kz[:, j * LANES:(j + 1) * LANES]

    lane = lax.broadcasted_iota(jnp.int32, (T, LANES), 1)
    z = _dot(xn, wf_ref[...]) + bf_ref[...]
    log_f = jnp.where(lane < FOX_HEADS, -LOG2E * _softplus(-z), 0.0)
    tri = tri_ref[...]
    d = sum(_dot(tri, part) for part in _split3(log_f))
    kb = sum(_dot(part, place_ref[n]) for n, part in enumerate(_split3(-d)))
    kb_ref[0] = kb.astype(bf16)

    carry = carry_ref[0:1, :]
    ident = (lax.broadcasted_iota(jnp.int32, (FOX_HEADS, LANES), 0)
             == lax.broadcasted_iota(jnp.int32, (FOX_HEADS, LANES), 1))
    col = jnp.sum(jnp.where(ident, jnp.broadcast_to(carry, (FOX_HEADS, LANES)), 0.0),
                  axis=1, keepdims=True)
    cref_ref[0, 0] = jnp.broadcast_to(col, (FOX_HEADS, T))
    carry_ref[...] = jnp.broadcast_to(carry + d[T - 1:T, :], (SUBLANES, LANES))

    fg = _dot(xn, wfg_ref[...])
    sg_ref[0] = (fg * _sigmoid(fg)).astype(bf16)
    gb_ref[0] = _sigmoid(_dot(xn, wmg_ref[...]) + bm_ref[...]).astype(bf16)


def _fox_proj(x, g_pre, wqt, wk, wvt, wf, bf, wfg, wmg, bm):
    B, S, _ = x.shape
    T = min(TILE, S)
    nt = S // T
    tri = jnp.asarray(np.tril(np.ones((T, T), np.float32)), bf16)
    place = np.zeros((3, LANES, LANES), np.float32)
    for n in range(3):
        for h in range(FOX_HEADS):
            place[n, h, n * FOX_HEADS + h] = 1.0
    place = jnp.asarray(place, bf16)
    full = lambda shape: pl.BlockSpec(shape, lambda b, t: (0,) * len(shape))
    tile = pl.BlockSpec((1, T, D), lambda b, t: (b, t, 0))
    tile_t = pl.BlockSpec((1, D, T), lambda b, t: (b, 0, t))
    vrows = FOX_HEADS * V_ROWS
    return pl.pallas_call(
        _fox_proj_kernel,
        grid=(B, nt),
        in_specs=[tile, full((1, D)), full((D, D)), full((D, D)), full((D, D)),
                  full((D, LANES)), full((1, LANES)), full((D, D)), full((D, D)), full((1, D)),
                  full((T, T)), full((3, LANES, LANES))],
        out_specs=[tile_t,
                   pl.BlockSpec((1, D // LANES, T, LANES), lambda b, t: (b, 0, t, 0)),
                   pl.BlockSpec((1, T, LANES), lambda b, t: (b, t, 0)),
                   pl.BlockSpec((1, vrows, T), lambda b, t: (b, 0, t)), tile, tile,
                   pl.BlockSpec((1, 1, FOX_HEADS, T), lambda b, t: (b, t, 0, 0))],
        out_shape=[jax.ShapeDtypeStruct((B, D, S), bf16),
                   jax.ShapeDtypeStruct((B, D // LANES, S, LANES), bf16),
                   jax.ShapeDtypeStruct((B, S, LANES), bf16),
                   jax.ShapeDtypeStruct((B, vrows, S), bf16),
                   jax.ShapeDtypeStruct((B, S, D), bf16),
                   jax.ShapeDtypeStruct((B, S, D), bf16),
                   jax.ShapeDtypeStruct((B, nt, FOX_HEADS, T), f32)],
        scratch_shapes=[pltpu.VMEM((SUBLANES, LANES), f32)],
        compiler_params=pltpu.CompilerParams(
            dimension_semantics=("arbitrary", "arbitrary"), vmem_limit_bytes=VMEM_LIMIT),
        name="fox_proj",
    )(x, g_pre, wqt, wk, wvt, wf, bf, wfg, wmg, bm, tri, place)


def _fox_attn_kernel(qi_ref, kj_ref, qt_ref, k_ref, kb_ref, vt_ref, cref_ref, sg_ref, gb_ref,
                     pam_ref, x_ref, wb_ref, wo_ref, gpost_ref, y_ref,
                     qa_ref, acc_ref, m_ref, s_ref, o_ref):
    T = x_ref.shape[1]
    step = pl.program_id(1)
    qi = qi_ref[step]
    kj = kj_ref[step]
    pair_rows = 2 * FOX_DH

    @pl.when(kj == 0)
    def _():
        rows = lax.broadcasted_iota(jnp.int32, (pair_rows, T), 0)
        for h in range(FOX_HEADS):
            j, e = divmod(h, 2)
            qpair = qt_ref[0, j * pair_rows:(j + 1) * pair_rows, :].astype(f32)
            mine = (rows >= FOX_DH) if e else (rows < FOX_DH)
            qa_ref[h, 0:pair_rows, :] = jnp.where(mine, qpair, 0.0).astype(bf16)
            sel = (rows == h) | (rows == FOX_HEADS + h) | (rows == 2 * FOX_HEADS + h)
            qa_ref[h, pair_rows:2 * pair_rows, :] = jnp.where(sel, 1.0, 0.0).astype(bf16)
        acc_ref[...] = jnp.zeros(acc_ref.shape, f32)
        m_ref[...] = jnp.full(m_ref.shape, NEG_INF, f32)

    def block(masked):
        kbias = kb_ref[0]
        if masked:
            causal = (lax.broadcasted_iota(jnp.int32, (T, T), 0)
                      <= lax.broadcasted_iota(jnp.int32, (T, T), 1))

        def scores(h, j, slot):
            lhs = jnp.concatenate([k_ref[0, j], kbias], axis=1)
            st = _dot(lhs, qa_ref[h])
            if masked:
                st = jnp.where(causal, st, NEG_INF)
            s_ref[slot] = st
            return jnp.max(st, axis=0, keepdims=True)

        def update(h, slot, bmax):
            cr = cref_ref[0, 0, h:h + 1, :]
            m_old = m_ref[h:h + 1, :]
            m_new = jnp.maximum(m_old, bmax - cr)
            alpha = jnp.exp2(m_old - m_new)
            p = jnp.exp2(s_ref[slot] - (m_new + cr)).astype(bf16)
            m_ref[h:h + 1, :] = m_new
            rows = slice(h * V_ROWS, (h + 1) * V_ROWS)
            pv = _dot(vt_ref[0, rows, :], p)
            acc_ref[rows, :] = alpha * acc_ref[rows, :] + pv

        bmax = scores(0, 0, 0)
        for h in range(FOX_HEADS):
            if h + 1 < FOX_HEADS:
                bmax_next = scores(h + 1, (h + 1) // 2, (h + 1) % 2)
            update(h, h % 2, bmax)
            bmax = bmax_next

    @pl.when(kj < qi)
    def _():
        block(False)

    @pl.when(kj == qi)
    def _():
        block(True)
        for h in range(FOX_HEADS):
            r0 = h * V_ROWS
            o_ref[h * FOX_DH:(h + 1) * FOX_DH, :] = (
                acc_ref[r0:r0 + FOX_DH, :] / acc_ref[r0 + FOX_DH:r0 + FOX_DH + 1, :])
        o = o_ref[...].T
        y_b = (o * sg_ref[0].astype(f32)).astype(bf16)
        merged = pam_ref[0].astype(f32) + gb_ref[0].astype(f32) * _dot(y_b, wb_ref[...])
        out = _dot(merged.astype(bf16), wo_ref[...])
        y_ref[0] = x_ref[0] + _rms(out, gpost_ref[...])


def _fox_attn(qt, k, kb, vt, cref, sg, gb, pam, x, wb, wo, g_post):
    B, S, _ = x.shape
    T = min(TILE, S)
    nt = S // T
    pairs = [(q, j) for q in range(nt) for j in range(q + 1)]
    qi_tab = jnp.asarray([p[0] for p in pairs], jnp.int32)
    kj_tab = jnp.asarray([p[1] for p in pairs], jnp.int32)
    full = lambda shape: pl.BlockSpec(shape, lambda b, s, qi, kj: (0,) * len(shape))
    qtile = pl.BlockSpec((1, T, D), lambda b, s, qi, kj: (b, qi[s], 0))
    grid_spec = pltpu.PrefetchScalarGridSpec(
        num_scalar_prefetch=2,
        grid=(B, len(pairs)),
        in_specs=[pl.BlockSpec((1, D, T), lambda b, s, qi, kj: (b, 0, qi[s])),
                  pl.BlockSpec((1, D // LANES, T, LANES), lambda b, s, qi, kj: (b, 0, kj[s], 0)),
                  pl.BlockSpec((1, T, LANES), lambda b, s, qi, kj: (b, kj[s], 0)),
                  pl.BlockSpec((1, FOX_HEADS * V_ROWS, T), lambda b, s, qi, kj: (b, 0, kj[s])),
                  pl.BlockSpec((1, 1, FOX_HEADS, T), lambda b, s, qi, kj: (b, kj[s], 0, 0)),
                  qtile, qtile, qtile, qtile, full((D, D)), full((D, D)), full((1, D))],
        out_specs=qtile,
        scratch_shapes=[pltpu.VMEM((FOX_HEADS, 4 * FOX_DH, T), bf16),
                        pltpu.VMEM((FOX_HEADS * V_ROWS, T), f32),
                        pltpu.VMEM((FOX_HEADS, T), f32),
                        pltpu.VMEM((2, T, T), f32),
                        pltpu.VMEM((D, T), f32)],
    )
    return pl.pallas_call(
        _fox_attn_kernel,
        grid_spec=grid_spec,
        out_shape=jax.ShapeDtypeStruct((B, S, D), f32),
        compiler_params=pltpu.CompilerParams(
            dimension_semantics=("arbitrary", "arbitrary"), vmem_limit_bytes=VMEM_LIMIT),
        name="fox_attn",
    )(qi_tab, kj_tab, qt, k, kb, vt, cref, sg, gb, pam, x, wb, wo, g_post)


def _block_diag_groups(w):
    nb, bw, _ = w.shape
    per = nb // LRU_GROUPS
    out = jnp.zeros((LRU_GROUPS, per * bw, per * bw), w.dtype)
    for g in range(LRU_GROUPS):
        for n in range(per):
            out = out.at[g, n * bw:(n + 1) * bw, n * bw:(n + 1) * bw].set(w[g * per + n])
    return out


def kernel(x, mem, g_pre, w_in, conv_w, conv_b, w_lru_r, b_lru_r, w_lru_i, b_lru_i, lru_lambda,
           b_forget, g_mem, w_mem_k, w_mem_v, w_branch, b_merge, w_out, g_post):
    row = lambda v: v.reshape(1, -1).astype(f32)
    sizes = (D, D, D, D, D, FOX_HEADS, D, D, D, 3 * D)
    offs = np.concatenate([[0], np.cumsum(sizes)])
    cols = [w_in[:, offs[n]:offs[n + 1]] for n in range(len(sizes))]
    w_ax, w_ag, w_fq, w_fk, w_fv, w_ff, w_fg, w_mq, w_mg, w_merge = cols
    w_mga, w_mgb, w_mgc = (w_merge[:, n * D:(n + 1) * D].astype(bf16) for n in range(3))
    w_ff = jnp.pad(w_ff, ((0, 0), (0, LANES - FOX_HEADS))).astype(bf16)
    b_f = jnp.pad(b_forget, (0, LANES - FOX_HEADS)).reshape(1, LANES).astype(f32)
    wb = w_branch.astype(bf16)
    g_pre_r = row(g_pre)

    mk, mv = _mem_kv(mem, row(g_mem), w_mem_k.astype(bf16), w_mem_v.astype(bf16))
    pa = _lru(x, g_pre_r, w_ax.astype(bf16), w_ag.astype(bf16), w_mga, conv_w.astype(f32),
              row(conv_b), _block_diag_groups(w_lru_r).astype(bf16),
              _block_diag_groups(w_lru_i).astype(bf16), row(b_lru_r), row(b_lru_i),
              row(lru_lambda), wb[0], row(b_merge[0]))
    pam = _memattn(x, g_pre_r, w_mq.astype(bf16), w_mg.astype(bf16), w_mgc, mk, mv, wb[2],
                   row(b_merge[2]), pa)
    qt, k, kb, vt, sg, gb, cref = _fox_proj(
        x, g_pre_r, w_fq.T.astype(bf16), w_fk.astype(bf16), w_fv.T.astype(bf16), w_ff, b_f,
        w_fg.astype(bf16), w_mgb, row(b_merge[1]))
    return _fox_attn(qt, k, kb, vt, cref, sg, gb, pam, x, wb[1], w_out.astype(bf16), row(g_post))
```

```python
import functools
import math

import numpy as np
import jax
import jax.numpy as jnp
from jax import lax
from jax.experimental import pallas as pl
from jax.experimental.pallas import tpu as pltpu

D = 1024
TILE = 512
FOX_HEADS = 16
FOX_DH = 64
MEM_HEADS = 4
MEM_DH = 256
LRU_GROUPS = 4
LRU_GW = D // LRU_GROUPS
LRU_C = 8.0
LRU_CHUNKS = 2
CONV_W = 4
RMS_EPS = 1e-6
NEG_INF = -1e30
LOG2E = math.log2(math.e)
V_ROWS = FOX_DH + 16
ATTN_AHEAD = 2
KV_SUB = 2
LANES = 128
SUBLANES = 8
VMEM_LIMIT = 56 * 1024 * 1024

f32 = jnp.float32
bf16 = jnp.bfloat16


def _dot(a, b):
    return jnp.dot(a, b, preferred_element_type=f32)


def _dot_nt(a, b):
    return lax.dot_general(a, b, (((1,), (1,)), ((), ())), preferred_element_type=f32)


def _rms(x, g):
    return x * lax.rsqrt(jnp.mean(x * x, axis=-1, keepdims=True) + RMS_EPS) * g


def _sigmoid(z):
    return 1.0 / (1.0 + jnp.exp2(z * -LOG2E))


def _softplus(z):
    return jnp.maximum(z, 0.0) + jnp.log1p(jnp.exp(-jnp.abs(z)))


def _split3(v):
    hi = v.astype(bf16)
    r1 = v - hi.astype(f32)
    mid = r1.astype(bf16)
    lo = (r1 - mid.astype(f32)).astype(bf16)
    return hi, mid, lo


def _mem_kv_kernel(mem_ref, g_ref, wk_ref, wv_ref, mk_ref, mv_ref):
    mn = _rms(mem_ref[0], g_ref[...]).astype(bf16)
    mk_ref[0] = _dot(mn, wk_ref[...]).astype(bf16)
    mv_ref[0] = _dot(mn, wv_ref[...]).astype(bf16)


def _mem_kv(mem, g_mem, wk, wv):
    B, M, _ = mem.shape
    full = lambda shape: pl.BlockSpec(shape, lambda b: (0,) * len(shape))
    return pl.pallas_call(
        _mem_kv_kernel,
        grid=(B,),
        in_specs=[pl.BlockSpec((1, M, D), lambda b: (b, 0, 0)), full((1, D)),
                  full((D, D)), full((D, D))],
        out_specs=[pl.BlockSpec((1, M, D), lambda b: (b, 0, 0))] * 2,
        out_shape=[jax.ShapeDtypeStruct((B, M, D), bf16)] * 2,
        compiler_params=pltpu.CompilerParams(
            dimension_semantics=("arbitrary",), vmem_limit_bytes=VMEM_LIMIT),
        name="mem_kv",
    )(mem, g_mem, wk, wv)


def _lru_kernel(x_ref, g_ref, wax_ref, wag_ref, wmg_ref, cw_ref, cb_ref, wr_ref, wi_ref,
                br_ref, bi_ref, lam_ref, wb_ref, bm_ref, pa_ref,
                xn_ref, xpad_ref, h_ref, hc_ref):
    NB, _, TT, _ = x_ref.shape
    R = NB * TT
    halo = (CONV_W - 1) * NB
    t = pl.program_id(1)

    @pl.when(t == 0)
    def _():
        xpad_ref[0:halo, :] = jnp.zeros((halo, D), f32)
        hc_ref[...] = jnp.zeros((NB, D), f32)

    x = jnp.swapaxes(x_ref[:, 0], 0, 1).reshape(R, D)
    xn_ref[...] = _rms(x, g_ref[...]).astype(bf16)

    RC = R // LRU_CHUNKS
    TC = TT // LRU_CHUNKS
    softplus_lam = _softplus(-lam_ref[...])

    def rows_of(c):
        return slice(c * RC, (c + 1) * RC)

    def conv_in(c):
        ax = _dot(xn_ref[rows_of(c), :], wax_ref[...])
        xpad_ref[halo + c * RC:halo + (c + 1) * RC, :] = ax

    def gates(c):
        xc = cb_ref[...] + cw_ref[0:1, :] * xpad_ref[c * RC:(c + 1) * RC, :]
        for k in range(1, CONV_W):
            xc = xc + cw_ref[k:k + 1, :] * xpad_ref[c * RC + k * NB:(c + 1) * RC + k * NB, :]
        xcb = xc.astype(bf16)
        r = jnp.concatenate(
            [_dot(xcb[:, g * LRU_GW:(g + 1) * LRU_GW], wr_ref[g]) for g in range(LRU_GROUPS)],
            axis=1)
        i = jnp.concatenate(
            [_dot(xcb[:, g * LRU_GW:(g + 1) * LRU_GW], wi_ref[g]) for g in range(LRU_GROUPS)],
            axis=1)
        return xc, r, i

    def recurrence(c, xc, r, i, h):
        r = _sigmoid(r + br_ref[...])
        i = _sigmoid(i + bi_ref[...])
        log_a = (-LRU_C * r) * softplus_lam
        a = jnp.exp(log_a)
        z = -jnp.tanh(log_a) * (a * a + 1.0)
        u = jnp.where(z > 0.0, z * lax.rsqrt(z), 0.0) * (i * xc)
        for s in range(TC):
            rows = slice(s * NB, (s + 1) * NB)
            h = a[rows, :] * h + u[rows, :]
            h_ref[c * RC + s * NB:c * RC + (s + 1) * NB, :] = h
        return h

    def project(c, ag, mg):
        y_a = (h_ref[rows_of(c), :] * (ag * _sigmoid(ag))).astype(bf16)
        pa = _sigmoid(mg + bm_ref[...]) * _dot(y_a, wb_ref[...])
        pa_ref[:, 0, c * TC:(c + 1) * TC, :] = (
            jnp.swapaxes(pa.reshape(TC, NB, D), 0, 1).astype(bf16))

    chunks = range(LRU_CHUNKS)
    for c in chunks:
        conv_in(c)
    staged = []
    for c in chunks:
        ag = _dot(xn_ref[rows_of(c), :], wag_ref[...])
        staged.append((ag,) + gates(c))
    xpad_ref[0:halo, :] = xpad_ref[R:R + halo, :]
    mgs = [_dot(xn_ref[rows_of(c), :], wmg_ref[...]) for c in chunks]
    h = hc_ref[...]
    for c in chunks:
        ag, xc, r, i = staged[c]
        h = recurrence(c, xc, r, i, h)
        project(c, ag, mgs[c])
    hc_ref[...] = h


def _lru(x, g_pre, wax, wag, wmg, conv_w, conv_b, wr, wi, br, bi, lam, wb, bm):
    B, S, _ = x.shape
    NB = SUBLANES
    assert B % NB == 0
    TT = min(TILE // NB, S)
    full = lambda shape: pl.BlockSpec(shape, lambda g, t: (0,) * len(shape))
    tile = pl.BlockSpec((NB, 1, TT, D), lambda g, t: (g, t, 0, 0))
    pa = pl.pallas_call(
        _lru_kernel,
        grid=(B // NB, S // TT),
        in_specs=[tile, full((1, D)), full((D, D)), full((D, D)), full((D, D)),
                  full((CONV_W, D)), full((1, D)),
                  full((LRU_GROUPS, LRU_GW, LRU_GW)), full((LRU_GROUPS, LRU_GW, LRU_GW)),
                  full((1, D)), full((1, D)), full((1, D)), full((D, D)), full((1, D))],
        out_specs=tile,
        out_shape=jax.ShapeDtypeStruct((B, S // TT, TT, D), bf16),
        scratch_shapes=[pltpu.VMEM((TT * NB, D), bf16),
                        pltpu.VMEM(((TT + CONV_W - 1) * NB, D), f32), pltpu.VMEM((TT * NB, D), f32),
                        pltpu.VMEM((NB, D), f32)],
        compiler_params=pltpu.CompilerParams(
            dimension_semantics=("arbitrary", "arbitrary"), vmem_limit_bytes=VMEM_LIMIT),
        name="lru",
    )(x.reshape(B, S // TT, TT, D), g_pre, wax, wag, wmg, conv_w, conv_b, wr, wi, br, bi, lam,
      wb, bm)
    return pa.reshape(B, S, D)


def _memattn_kernel(x_ref, g_ref, wq_ref, wg_ref, wmg_ref, mk_ref, mv_ref, wb_ref, bm_ref,
                    pa_ref, out_ref):
    xn = _rms(x_ref[0], g_ref[...]).astype(bf16)
    q = (_dot(xn, wq_ref[...]) * (1.0 / math.sqrt(MEM_DH))).astype(bf16)
    outs = []
    for h in range(MEM_HEADS):
        sl = slice(h * MEM_DH, (h + 1) * MEM_DH)
        s = _dot_nt(q[:, sl], mk_ref[0, :, sl])
        p = jnp.exp(s - jnp.max(s, axis=-1, keepdims=True))
        l = jnp.sum(p, axis=-1, keepdims=True)
        outs.append(_dot(p.astype(bf16), mv_ref[0, :, sl]) / l)
    o = jnp.concatenate(outs, axis=1)
    mg = _dot(xn, wg_ref[...])
    y_m = (o * (mg * _sigmoid(mg))).astype(bf16)
    gate = _sigmoid(_dot(xn, wmg_ref[...]) + bm_ref[...])
    out_ref[0] = (pa_ref[0].astype(f32) + gate * _dot(y_m, wb_ref[...])).astype(bf16)


def _memattn(x, g_pre, wq, wg, wmg, mk, mv, wb, bm, pa):
    B, S, _ = x.shape
    M = mk.shape[1]
    T = min(TILE, S)
    full = lambda shape: pl.BlockSpec(shape, lambda b, t: (0,) * len(shape))
    tile = pl.BlockSpec((1, T, D), lambda b, t: (b, t, 0))
    memb = pl.BlockSpec((1, M, D), lambda b, t: (b, 0, 0))
    return pl.pallas_call(
        _memattn_kernel,
        grid=(B, S // T),
        in_specs=[tile, full((1, D)), full((D, D)), full((D, D)), full((D, D)), memb, memb,
                  full((D, D)), full((1, D)), tile],
        out_specs=tile,
        out_shape=jax.ShapeDtypeStruct((B, S, D), bf16),
        compiler_params=pltpu.CompilerParams(
            dimension_semantics=("arbitrary", "arbitrary"), vmem_limit_bytes=VMEM_LIMIT),
        name="memattn",
    )(x, g_pre, wq, wg, wmg, mk, mv, wb, bm, pa)


def _fox_proj_kernel(x_ref, g_ref, wqt_ref, wk_ref, wvt_ref, wf_ref, bf_ref, wfg_ref, wmg_ref,
                     bm_ref, tri_ref, place_ref,
                     qt_ref, k_ref, kb_ref, vt_ref, sg_ref, gb_ref, cref_ref, carry_ref):
    T = x_ref.shape[1]
    t = pl.program_id(1)

    @pl.when(t == 0)
    def _():
        carry_ref[...] = jnp.zeros((SUBLANES, LANES), f32)

    xn = _rms(x_ref[0], g_ref[...]).astype(bf16)

    lane = lax.broadcasted_iota(jnp.int32, (T, LANES), 1)
    z = _dot(xn, wf_ref[...]) + bf_ref[...]
    qt_ref[0] = (_dot_nt(wqt_ref[...], xn) * (LOG2E / math.sqrt(FOX_DH))).astype(bf16)
    log_f = jnp.where(lane < FOX_HEADS, -LOG2E * _softplus(-z), 0.0)
    tri = tri_ref[...]
    d = sum(_dot(tri, part) for part in _split3(log_f))
    vt = _dot_nt(wvt_ref[...], xn).astype(bf16)
    ones = jnp.ones((V_ROWS - FOX_DH, T), bf16)
    for h in range(FOX_HEADS):
        vt_ref[0, h * V_ROWS:h * V_ROWS + FOX_DH, :] = vt[h * FOX_DH:(h + 1) * FOX_DH, :]
        vt_ref[0, h * V_ROWS + FOX_DH:(h + 1) * V_ROWS, :] = ones
    kb = sum(_dot(part, place_ref[n]) for n, part in enumerate(_split3(-d)))
    kb_ref[0] = kb.astype(bf16)
    kz = _dot(xn, wk_ref[...]).astype(bf16)
    for j in range(D // LANES):
        k_ref[0, j] = kz[:, j * LANES:(j + 1) * LANES]

    carry = carry_ref[0:1, :]
    ident = (lax.broadcasted_iota(jnp.int32, (FOX_HEADS, LANES), 0)
             == lax.broadcasted_iota(jnp.int32, (FOX_HEADS, LANES), 1))
    col = jnp.sum(jnp.where(ident, jnp.broadcast_to(carry, (FOX_HEADS, LANES)), 0.0),
                  axis=1, keepdims=True)
    cref_ref[0, 0] = jnp.broadcast_to(col, (FOX_HEADS, T))
    carry_ref[...] = jnp.broadcast_to(carry + d[T - 1:T, :], (SUBLANES, LANES))

    fg = _dot(xn, wfg_ref[...])
    sg_ref[0] = (fg * _sigmoid(fg)).astype(bf16)
    gb_ref[0] = _sigmoid(_dot(xn, wmg_ref[...]) + bm_ref[...]).astype(bf16)


def _fox_proj(x, g_pre, wqt, wk, wvt, wf, bf, wfg, wmg, bm):
    B, S, _ = x.shape
    T = min(TILE, S)
    nt = S // T
    tri = jnp.asarray(np.tril(np.ones((T, T), np.float32)), bf16)
    place = np.zeros((3, LANES, LANES), np.float32)
    for n in range(3):
        for h in range(FOX_HEADS):
            place[n, h, n * FOX_HEADS + h] = 1.0
    place = jnp.asarray(place, bf16)
    full = lambda shape: pl.BlockSpec(shape, lambda b, t: (0,) * len(shape))
    tile = pl.BlockSpec((1, T, D), lambda b, t: (b, t, 0))
    tile_t = pl.BlockSpec((1, D, T), lambda b, t: (b, 0, t))
    vrows = FOX_HEADS * V_ROWS
    return pl.pallas_call(
        _fox_proj_kernel,
        grid=(B, nt),
        in_specs=[tile, full((1, D)), full((D, D)), full((D, D)), full((D, D)),
                  full((D, LANES)), full((1, LANES)), full((D, D)), full((D, D)), full((1, D)),
                  full((T, T)), full((3, LANES, LANES))],
        out_specs=[tile_t,
                   pl.BlockSpec((1, D // LANES, T, LANES), lambda b, t: (b, 0, t, 0)),
                   pl.BlockSpec((1, T, LANES), lambda b, t: (b, t, 0)),
                   pl.BlockSpec((1, vrows, T), lambda b, t: (b, 0, t)), tile, tile,
                   pl.BlockSpec((1, 1, FOX_HEADS, T), lambda b, t: (b, t, 0, 0))],
        out_shape=[jax.ShapeDtypeStruct((B, D, S), bf16),
                   jax.ShapeDtypeStruct((B, D // LANES, S, LANES), bf16),
                   jax.ShapeDtypeStruct((B, S, LANES), bf16),
                   jax.ShapeDtypeStruct((B, vrows, S), bf16),
                   jax.ShapeDtypeStruct((B, S, D), bf16),
                   jax.ShapeDtypeStruct((B, S, D), bf16),
                   jax.ShapeDtypeStruct((B, nt, FOX_HEADS, T), f32)],
        scratch_shapes=[pltpu.VMEM((SUBLANES, LANES), f32)],
        compiler_params=pltpu.CompilerParams(
            dimension_semantics=("arbitrary", "arbitrary"), vmem_limit_bytes=VMEM_LIMIT),
        name="fox_proj",
    )(x, g_pre, wqt, wk, wvt, wf, bf, wfg, wmg, bm, tri, place)


def _fox_attn_kernel(qi_ref, km_ref, kind_ref, qt_ref, k_ref, kb_ref, vt_ref, cref_ref, sg_ref,
                     gb_ref, pam_ref, x_ref, wb_ref, wo_ref, gpost_ref, y_ref,
                     qa_ref, acc_ref, m_ref, s_ref, o_ref):
    T = x_ref.shape[1]
    step = pl.program_id(1)
    kind = kind_ref[step]
    pair_rows = 2 * FOX_DH

    @pl.when(km_ref[step] == 0)
    def _():
        rows = lax.broadcasted_iota(jnp.int32, (pair_rows, T), 0)
        for h in range(FOX_HEADS):
            j, e = divmod(h, 2)
            qpair = qt_ref[0, j * pair_rows:(j + 1) * pair_rows, :].astype(f32)
            mine = (rows >= FOX_DH) if e else (rows < FOX_DH)
            qa_ref[h, 0:pair_rows, :] = jnp.where(mine, qpair, 0.0).astype(bf16)
            sel = (rows == h) | (rows == FOX_HEADS + h) | (rows == 2 * FOX_HEADS + h)
            qa_ref[h, pair_rows:2 * pair_rows, :] = jnp.where(sel, 1.0, 0.0).astype(bf16)
        acc_ref[...] = jnp.zeros(acc_ref.shape, f32)
        m_ref[...] = jnp.full(m_ref.shape, NEG_INF, f32)

    def run(tiles):
        work = [(sub, h, masked) for sub, masked in tiles for h in range(FOX_HEADS)]
        causal = (lax.broadcasted_iota(jnp.int32, (T, T), 0)
                  <= lax.broadcasted_iota(jnp.int32, (T, T), 1))

        def scores(sub, h, masked, slot):
            keys = slice(sub * T, (sub + 1) * T)
            lhs = jnp.concatenate([k_ref[0, h // 2, keys, :], kb_ref[0, keys, :]], axis=1)
            st = _dot(lhs, qa_ref[h])
            if masked:
                st = jnp.where(causal, st, NEG_INF)
            s_ref[slot] = st
            return jnp.max(st, axis=0, keepdims=True)

        def update(sub, h, slot, bmax):
            cr = cref_ref[0, sub, h:h + 1, :]
            m_old = m_ref[h:h + 1, :]
            m_new = jnp.maximum(m_old, bmax - cr)
            alpha = jnp.exp2(m_old - m_new)
            p = jnp.exp2(s_ref[slot] - (m_new + cr)).astype(bf16)
            m_ref[h:h + 1, :] = m_new
            rows = slice(h * V_ROWS, (h + 1) * V_ROWS)
            pv = _dot(vt_ref[0, rows, sub * T:(sub + 1) * T], p)
            acc_ref[rows, :] = alpha * acc_ref[rows, :] + pv

        slots = s_ref.shape[0]
        bmax = {}
        for n in range(len(work) + ATTN_AHEAD):
            if n < len(work):
                sub, h, masked = work[n]
                bmax[n] = scores(sub, h, masked, n % slots)
            d = n - ATTN_AHEAD
            if d >= 0:
                sub, h, _ = work[d]
                update(sub, h, d % slots, bmax.pop(d))

    @pl.when(kind == 0)
    def _():
        run([(sub, False) for sub in range(KV_SUB)])

    @pl.when(kind == 1)
    def _():
        run([(0, True)])

    @pl.when(kind == 2)
    def _():
        run([(0, False), (1, True)])

    @pl.when(kind != 0)
    def _():
        for h in range(FOX_HEADS):
            r0 = h * V_ROWS
            o_ref[h * FOX_DH:(h + 1) * FOX_DH, :] = (
                acc_ref[r0:r0 + FOX_DH, :] / acc_ref[r0 + FOX_DH:r0 + FOX_DH + 1, :])
        o = o_ref[...].T
        y_b = (o * sg_ref[0].astype(f32)).astype(bf16)
        merged = pam_ref[0].astype(f32) + gb_ref[0].astype(f32) * _dot(y_b, wb_ref[...])
        out = _dot(merged.astype(bf16), wo_ref[...])
        y_ref[0] = x_ref[0] + _rms(out, gpost_ref[...])


def _fox_attn(qt, k, kb, vt, cref, sg, gb, pam, x, wb, wo, g_post):
    B, S, _ = x.shape
    T = min(TILE, S)
    nt = S // T
    assert nt % KV_SUB == 0
    steps = []
    for i in range(nt):
        for m in range(i // KV_SUB + 1):
            kind = 0 if m < i // KV_SUB else (1 if i % KV_SUB == 0 else 2)
            steps.append((i, m, kind))
    qi_tab, km_tab, kind_tab = (jnp.asarray([s[n] for s in steps], jnp.int32) for n in range(3))
    TK = KV_SUB * T
    full = lambda shape: pl.BlockSpec(shape, lambda b, s, qi, km, kind: (0,) * len(shape))
    qtile = pl.BlockSpec((1, T, D), lambda b, s, qi, km, kind: (b, qi[s], 0))
    grid_spec = pltpu.PrefetchScalarGridSpec(
        num_scalar_prefetch=3,
        grid=(B, len(steps)),
        in_specs=[pl.BlockSpec((1, D, T), lambda b, s, qi, km, kind: (b, 0, qi[s])),
                  pl.BlockSpec((1, D // LANES, TK, LANES),
                               lambda b, s, qi, km, kind: (b, 0, km[s], 0)),
                  pl.BlockSpec((1, TK, LANES), lambda b, s, qi, km, kind: (b, km[s], 0)),
                  pl.BlockSpec((1, FOX_HEADS * V_ROWS, TK),
                               lambda b, s, qi, km, kind: (b, 0, km[s])),
                  pl.BlockSpec((1, KV_SUB, FOX_HEADS, T),
                               lambda b, s, qi, km, kind: (b, km[s], 0, 0)),
                  qtile, qtile, qtile, qtile, full((D, D)), full((D, D)), full((1, D))],
        out_specs=qtile,
        scratch_shapes=[pltpu.VMEM((FOX_HEADS, 4 * FOX_DH, T), bf16),
                        pltpu.VMEM((FOX_HEADS * V_ROWS, T), f32),
                        pltpu.VMEM((FOX_HEADS, T), f32),
                        pltpu.VMEM((ATTN_AHEAD + 1, T, T), f32),
                        pltpu.VMEM((D, T), f32)],
    )
    return pl.pallas_call(
        _fox_attn_kernel,
        grid_spec=grid_spec,
        out_shape=jax.ShapeDtypeStruct((B, S, D), f32),
        compiler_params=pltpu.CompilerParams(
            dimension_semantics=("arbitrary", "arbitrary"), vmem_limit_bytes=VMEM_LIMIT),
        name="fox_attn",
    )(qi_tab, km_tab, kind_tab, qt, k, kb, vt, cref, sg, gb, pam, x, wb, wo, g_post)


def _block_diag_groups(w):
    nb, bw, _ = w.shape
    per = nb // LRU_GROUPS
    out = jnp.zeros((LRU_GROUPS, per * bw, per * bw), w.dtype)
    for g in range(LRU_GROUPS):
        for n in range(per):
            out = out.at[g, n * bw:(n + 1) * bw, n * bw:(n + 1) * bw].set(w[g * per + n])
    return out


def kernel(x, mem, g_pre, w_in, conv_w, conv_b, w_lru_r, b_lru_r, w_lru_i, b_lru_i, lru_lambda,
           b_forget, g_mem, w_mem_k, w_mem_v, w_branch, b_merge, w_out, g_post):
    row = lambda v: v.reshape(1, -1).astype(f32)
    sizes = (D, D, D, D, D, FOX_HEADS, D, D, D, 3 * D)
    offs = np.concatenate([[0], np.cumsum(sizes)])
    cols = [w_in[:, offs[n]:offs[n + 1]] for n in range(len(sizes))]
    w_ax, w_ag, w_fq, w_fk, w_fv, w_ff, w_fg, w_mq, w_mg, w_merge = cols
    w_mga, w_mgb, w_mgc = (w_merge[:, n * D:(n + 1) * D].astype(bf16) for n in range(3))
    w_ff = jnp.pad(w_ff, ((0, 0), (0, LANES - FOX_HEADS))).astype(bf16)
    b_f = jnp.pad(b_forget, (0, LANES - FOX_HEADS)).reshape(1, LANES).astype(f32)
    wb = w_branch.astype(bf16)
    g_pre_r = row(g_pre)

    mk, mv = _mem_kv(mem, row(g_mem), w_mem_k.astype(bf16), w_mem_v.astype(bf16))
    pa = _lru(x, g_pre_r, w_ax.astype(bf16), w_ag.astype(bf16), w_mga, conv_w.astype(f32),
              row(conv_b), _block_diag_groups(w_lru_r).astype(bf16),
              _block_diag_groups(w_lru_i).astype(bf16), row(b_lru_r), row(b_lru_i),
              row(lru_lambda), wb[0], row(b_merge[0]))
    pam = _memattn(x, g_pre_r, w_mq.astype(bf16), w_mg.astype(bf16), w_mgc, mk, mv, wb[2],
                   row(b_merge[2]), pa)
    qt, k, kb, vt, sg, gb, cref = _fox_proj(
        x, g_pre_r, w_fq.T.astype(bf16), w_fk.astype(bf16), w_fv.T.astype(bf16), w_ff, b_f,
        w_fg.astype(bf16), w_mgb, row(b_merge[1]))
    return _fox_attn(qt, k, kb, vt, cref, sg, gb, pam, x, wb[1], w_out.astype(bf16), row(g_post))
```

```python
import functools
import math

import numpy as np
import jax
import jax.numpy as jnp
from jax import lax
from jax.experimental import pallas as pl
from jax.experimental.pallas import tpu as pltpu

D = 1024
TILE = 512
FOX_HEADS = 16
FOX_DH = 64
MEM_HEADS = 4
MEM_DH = 256
LRU_GROUPS = 4
LRU_GW = D // LRU_GROUPS
LRU_C = 8.0
LRU_CHUNKS = 2
CONV_W = 4
RMS_EPS = 1e-6
NEG_INF = -1e30
LOG2E = math.log2(math.e)
V_ROWS = FOX_DH + 16
ATTN_AHEAD = 2
KV_SUB = 2
LANES = 128
SUBLANES = 8
VMEM_LIMIT = 56 * 1024 * 1024

f32 = jnp.float32
bf16 = jnp.bfloat16


def _dot(a, b):
    return jnp.dot(a, b, preferred_element_type=f32)


def _dot_nt(a, b):
    return lax.dot_general(a, b, (((1,), (1,)), ((), ())), preferred_element_type=f32)


def _rms(x, g):
    return x * lax.rsqrt(jnp.mean(x * x, axis=-1, keepdims=True) + RMS_EPS) * g


def _sigmoid(z):
    return 1.0 / (1.0 + jnp.exp2(z * -LOG2E))


def _softplus(z):
    return jnp.maximum(z, 0.0) + jnp.log1p(jnp.exp(-jnp.abs(z)))


def _split3(v):
    hi = v.astype(bf16)
    r1 = v - hi.astype(f32)
    mid = r1.astype(bf16)
    lo = (r1 - mid.astype(f32)).astype(bf16)
    return hi, mid, lo


def _mem_kv_kernel(mem_ref, g_ref, wk_ref, wv_ref, mk_ref, mv_ref):
    mn = _rms(mem_ref[0], g_ref[...]).astype(bf16)
    mk_ref[0] = _dot(mn, wk_ref[...]).astype(bf16)
    mv_ref[0] = _dot(mn, wv_ref[...]).astype(bf16)


def _mem_kv(mem, g_mem, wk, wv):
    B, M, _ = mem.shape
    full = lambda shape: pl.BlockSpec(shape, lambda b: (0,) * len(shape))
    return pl.pallas_call(
        _mem_kv_kernel,
        grid=(B,),
        in_specs=[pl.BlockSpec((1, M, D), lambda b: (b, 0, 0)), full((1, D)),
                  full((D, D)), full((D, D))],
        out_specs=[pl.BlockSpec((1, M, D), lambda b: (b, 0, 0))] * 2,
        out_shape=[jax.ShapeDtypeStruct((B, M, D), bf16)] * 2,
        compiler_params=pltpu.CompilerParams(
            dimension_semantics=("arbitrary",), vmem_limit_bytes=VMEM_LIMIT),
        name="mem_kv",
    )(mem, g_mem, wk, wv)


def _lru_kernel(x_ref, g_ref, wax_ref, wag_ref, wmg_ref, cw_ref, cb_ref, wr_ref, wi_ref,
                br_ref, bi_ref, lam_ref, wb_ref, bm_ref, pa_ref,
                xn_ref, xpad_ref, h_ref, hc_ref):
    NB, _, TT, _ = x_ref.shape
    R = NB * TT
    halo = (CONV_W - 1) * NB
    t = pl.program_id(1)

    @pl.when(t == 0)
    def _():
        xpad_ref[0:halo, :] = jnp.zeros((halo, D), f32)
        hc_ref[...] = jnp.zeros((NB, D), f32)

    x = jnp.swapaxes(x_ref[:, 0], 0, 1).reshape(R, D)
    xn_ref[...] = _rms(x, g_ref[...]).astype(bf16)

    RC = R // LRU_CHUNKS
    TC = TT // LRU_CHUNKS
    softplus_lam = _softplus(-lam_ref[...])

    def rows_of(c):
        return slice(c * RC, (c + 1) * RC)

    def conv_in(c):
        ax = _dot(xn_ref[rows_of(c), :], wax_ref[...])
        xpad_ref[halo + c * RC:halo + (c + 1) * RC, :] = ax

    def gates(c):
        xc = cb_ref[...] + cw_ref[0:1, :] * xpad_ref[c * RC:(c + 1) * RC, :]
        for k in range(1, CONV_W):
            xc = xc + cw_ref[k:k + 1, :] * xpad_ref[c * RC + k * NB:(c + 1) * RC + k * NB, :]
        xcb = xc.astype(bf16)
        r = jnp.concatenate(
            [_dot(xcb[:, g * LRU_GW:(g + 1) * LRU_GW], wr_ref[g]) for g in range(LRU_GROUPS)],
            axis=1)
        i = jnp.concatenate(
            [_dot(xcb[:, g * LRU_GW:(g + 1) * LRU_GW], wi_ref[g]) for g in range(LRU_GROUPS)],
            axis=1)
        return xc, r, i

    def recurrence(c, xc, r, i, h):
        r = _sigmoid(r + br_ref[...])
        i = _sigmoid(i + bi_ref[...])
        log_a = (-LRU_C * r) * softplus_lam
        a = jnp.exp(log_a)
        z = -jnp.tanh(log_a) * (a * a + 1.0)
        u = jnp.where(z > 0.0, z * lax.rsqrt(z), 0.0) * (i * xc)
        for s in range(TC):
            rows = slice(s * NB, (s + 1) * NB)
            h = a[rows, :] * h + u[rows, :]
            h_ref[c * RC + s * NB:c * RC + (s + 1) * NB, :] = h
        return h

    def project(c, ag, mg):
        y_a = (h_ref[rows_of(c), :] * (ag * _sigmoid(ag))).astype(bf16)
        pa = _sigmoid(mg + bm_ref[...]) * _dot(y_a, wb_ref[...])
        pa_ref[:, 0, c * TC:(c + 1) * TC, :] = (
            jnp.swapaxes(pa.reshape(TC, NB, D), 0, 1).astype(bf16))

    chunks = range(LRU_CHUNKS)
    for c in chunks:
        conv_in(c)
    staged = []
    for c in chunks:
        ag = _dot(xn_ref[rows_of(c), :], wag_ref[...])
        staged.append((ag,) + gates(c))
    xpad_ref[0:halo, :] = xpad_ref[R:R + halo, :]
    mgs = [_dot(xn_ref[rows_of(c), :], wmg_ref[...]) for c in chunks]
    h = hc_ref[...]
    for c in chunks:
        ag, xc, r, i = staged[c]
        h = recurrence(c, xc, r, i, h)
        project(c, ag, mgs[c])
    hc_ref[...] = h


def _lru(x, g_pre, wax, wag, wmg, conv_w, conv_b, wr, wi, br, bi, lam, wb, bm):
    B, S, _ = x.shape
    NB = SUBLANES
    assert B % NB == 0
    TT = min(TILE // NB, S)
    full = lambda shape: pl.BlockSpec(shape, lambda g, t: (0,) * len(shape))
    tile = pl.BlockSpec((NB, 1, TT, D), lambda g, t: (g, t, 0, 0))
    pa = pl.pallas_call(
        _lru_kernel,
        grid=(B // NB, S // TT),
        in_specs=[tile, full((1, D)), full((D, D)), full((D, D)), full((D, D)),
                  full((CONV_W, D)), full((1, D)),
                  full((LRU_GROUPS, LRU_GW, LRU_GW)), full((LRU_GROUPS, LRU_GW, LRU_GW)),
                  full((1, D)), full((1, D)), full((1, D)), full((D, D)), full((1, D))],
        out_specs=tile,
        out_shape=jax.ShapeDtypeStruct((B, S // TT, TT, D), bf16),
        scratch_shapes=[pltpu.VMEM((TT * NB, D), bf16),
                        pltpu.VMEM(((TT + CONV_W - 1) * NB, D), f32), pltpu.VMEM((TT * NB, D), f32),
                        pltpu.VMEM((NB, D), f32)],
        compiler_params=pltpu.CompilerParams(
            dimension_semantics=("arbitrary", "arbitrary"), vmem_limit_bytes=VMEM_LIMIT),
        name="lru",
    )(x.reshape(B, S // TT, TT, D), g_pre, wax, wag, wmg, conv_w, conv_b, wr, wi, br, bi, lam,
      wb, bm)
    return pa.reshape(B, S, D)


def _memattn_kernel(x_ref, g_ref, wq_ref, wg_ref, wmg_ref, mk_ref, mv_ref, wb_ref, bm_ref,
                    pa_ref, out_ref):
    xn = _rms(x_ref[0], g_ref[...]).astype(bf16)
    q = (_dot(xn, wq_ref[...]) * (1.0 / math.sqrt(MEM_DH))).astype(bf16)
    outs = []
    for h in range(MEM_HEADS):
        sl = slice(h * MEM_DH, (h + 1) * MEM_DH)
        s = _dot_nt(q[:, sl], mk_ref[0, :, sl])
        p = jnp.exp(s - jnp.max(s, axis=-1, keepdims=True))
        l = jnp.sum(p, axis=-1, keepdims=True)
        outs.append(_dot(p.astype(bf16), mv_ref[0, :, sl]) / l)
    o = jnp.concatenate(outs, axis=1)
    mg = _dot(xn, wg_ref[...])
    y_m = (o * (mg * _sigmoid(mg))).astype(bf16)
    gate = _sigmoid(_dot(xn, wmg_ref[...]) + bm_ref[...])
    out_ref[0] = (pa_ref[0].astype(f32) + gate * _dot(y_m, wb_ref[...])).astype(bf16)


def _memattn(x, g_pre, wq, wg, wmg, mk, mv, wb, bm, pa):
    B, S, _ = x.shape
    M = mk.shape[1]
    T = min(TILE, S)
    full = lambda shape: pl.BlockSpec(shape, lambda b, t: (0,) * len(shape))
    tile = pl.BlockSpec((1, T, D), lambda b, t: (b, t, 0))
    memb = pl.BlockSpec((1, M, D), lambda b, t: (b, 0, 0))
    return pl.pallas_call(
        _memattn_kernel,
        grid=(B, S // T),
        in_specs=[tile, full((1, D)), full((D, D)), full((D, D)), full((D, D)), memb, memb,
                  full((D, D)), full((1, D)), tile],
        out_specs=tile,
        out_shape=jax.ShapeDtypeStruct((B, S, D), bf16),
        compiler_params=pltpu.CompilerParams(
            dimension_semantics=("arbitrary", "arbitrary"), vmem_limit_bytes=VMEM_LIMIT),
        name="memattn",
    )(x, g_pre, wq, wg, wmg, mk, mv, wb, bm, pa)


def _fox_proj_kernel(x_ref, g_ref, wqt_ref, wk_ref, wvt_ref, wf_ref, bf_ref, wfg_ref, wmg_ref,
                     bm_ref,
                     qt_ref, k_ref, kb_ref, vt_ref, sg_ref, gb_ref, cref_ref, carry_ref, d_ref):
    T = x_ref.shape[1]
    t = pl.program_id(1)

    @pl.when(t == 0)
    def _():
        carry_ref[...] = jnp.zeros((SUBLANES, LANES), f32)

    xn = _rms(x_ref[0], g_ref[...]).astype(bf16)

    lane = lax.broadcasted_iota(jnp.int32, (T, LANES), 1)
    z = _dot(xn, wf_ref[...]) + bf_ref[...]
    qt_ref[0] = (_dot_nt(wqt_ref[...], xn) * (LOG2E / math.sqrt(FOX_DH))).astype(bf16)
    log_f = jnp.where(lane < FOX_HEADS, -LOG2E * _softplus(-z), 0.0)
    row = lax.broadcasted_iota(jnp.int32, (T, LANES), 0) % SUBLANES
    scan = log_f
    shift = 1
    while shift < SUBLANES:
        scan = scan + jnp.where(row >= shift, pltpu.roll(scan, shift, 0), 0.0)
        shift *= 2
    d_ref[...] = scan
    run = jnp.zeros((SUBLANES, LANES), f32)
    for grp in range(T // SUBLANES):
        rows = slice(grp * SUBLANES, (grp + 1) * SUBLANES)
        blk = d_ref[rows, :] + run
        d_ref[rows, :] = blk
        run = jnp.broadcast_to(blk[SUBLANES - 1:SUBLANES, :], (SUBLANES, LANES))
    d = d_ref[...]
    vt = _dot_nt(wvt_ref[...], xn).astype(bf16)
    ones = jnp.ones((V_ROWS - FOX_DH, T), bf16)
    for h in range(FOX_HEADS):
        vt_ref[0, h * V_ROWS:h * V_ROWS + FOX_DH, :] = vt[h * FOX_DH:(h + 1) * FOX_DH, :]
        vt_ref[0, h * V_ROWS + FOX_DH:(h + 1) * V_ROWS, :] = ones
    hi, mid, lo = _split3(-d)
    kb = (hi.astype(f32) + pltpu.roll(mid.astype(f32), FOX_HEADS, 1)
          + pltpu.roll(lo.astype(f32), 2 * FOX_HEADS, 1))
    kb_ref[0] = kb.astype(bf16)
    kz = _dot(xn, wk_ref[...]).astype(bf16)
    for j in range(D // LANES):
        k_ref[0, j] = kz[:, j * LANES:(j + 1) * LANES]

    carry = carry_ref[0:1, :]
    ident = (lax.broadcasted_iota(jnp.int32, (FOX_HEADS, LANES), 0)
             == lax.broadcasted_iota(jnp.int32, (FOX_HEADS, LANES), 1))
    col = jnp.sum(jnp.where(ident, jnp.broadcast_to(carry, (FOX_HEADS, LANES)), 0.0),
                  axis=1, keepdims=True)
    cref_ref[0, 0] = jnp.broadcast_to(col, (FOX_HEADS, T))
    carry_ref[...] = jnp.broadcast_to(carry + d[T - 1:T, :], (SUBLANES, LANES))

    fg = _dot(xn, wfg_ref[...])
    sg_ref[0] = (fg * _sigmoid(fg)).astype(bf16)
    gb_ref[0] = _sigmoid(_dot(xn, wmg_ref[...]) + bm_ref[...]).astype(bf16)


def _fox_proj(x, g_pre, wqt, wk, wvt, wf, bf, wfg, wmg, bm):
    B, S, _ = x.shape
    T = min(TILE, S)
    nt = S // T
    full = lambda shape: pl.BlockSpec(shape, lambda b, t: (0,) * len(shape))
    tile = pl.BlockSpec((1, T, D), lambda b, t: (b, t, 0))
    tile_t = pl.BlockSpec((1, D, T), lambda b, t: (b, 0, t))
    vrows = FOX_HEADS * V_ROWS
    return pl.pallas_call(
        _fox_proj_kernel,
        grid=(B, nt),
        in_specs=[tile, full((1, D)), full((D, D)), full((D, D)), full((D, D)),
                  full((D, LANES)), full((1, LANES)), full((D, D)), full((D, D)), full((1, D))],
        out_specs=[tile_t,
                   pl.BlockSpec((1, D // LANES, T, LANES), lambda b, t: (b, 0, t, 0)),
                   pl.BlockSpec((1, T, LANES), lambda b, t: (b, t, 0)),
                   pl.BlockSpec((1, vrows, T), lambda b, t: (b, 0, t)), tile, tile,
                   pl.BlockSpec((1, 1, FOX_HEADS, T), lambda b, t: (b, t, 0, 0))],
        out_shape=[jax.ShapeDtypeStruct((B, D, S), bf16),
                   jax.ShapeDtypeStruct((B, D // LANES, S, LANES), bf16),
                   jax.ShapeDtypeStruct((B, S, LANES), bf16),
                   jax.ShapeDtypeStruct((B, vrows, S), bf16),
                   jax.ShapeDtypeStruct((B, S, D), bf16),
                   jax.ShapeDtypeStruct((B, S, D), bf16),
                   jax.ShapeDtypeStruct((B, nt, FOX_HEADS, T), f32)],
        scratch_shapes=[pltpu.VMEM((SUBLANES, LANES), f32), pltpu.VMEM((T, LANES), f32)],
        compiler_params=pltpu.CompilerParams(
            dimension_semantics=("arbitrary", "arbitrary"), vmem_limit_bytes=VMEM_LIMIT),
        name="fox_proj",
    )(x, g_pre, wqt, wk, wvt, wf, bf, wfg, wmg, bm)


def _fox_attn_kernel(qi_ref, km_ref, kind_ref, qt_ref, k_ref, kb_ref, vt_ref, cref_ref, sg_ref,
                     gb_ref, pam_ref, x_ref, wb_ref, wo_ref, gpost_ref, y_ref,
                     qa_ref, acc_ref, m_ref, s_ref, o_ref):
    T = x_ref.shape[1]
    step = pl.program_id(1)
    kind = kind_ref[step]
    pair_rows = 2 * FOX_DH

    @pl.when(km_ref[step] == 0)
    def _():
        rows = lax.broadcasted_iota(jnp.int32, (pair_rows, T), 0)
        for h in range(FOX_HEADS):
            j, e = divmod(h, 2)
            qpair = qt_ref[0, j * pair_rows:(j + 1) * pair_rows, :].astype(f32)
            mine = (rows >= FOX_DH) if e else (rows < FOX_DH)
            qa_ref[h, 0:pair_rows, :] = jnp.where(mine, qpair, 0.0).astype(bf16)
            sel = (rows == h) | (rows == FOX_HEADS + h) | (rows == 2 * FOX_HEADS + h)
            qa_ref[h, pair_rows:2 * pair_rows, :] = jnp.where(sel, 1.0, 0.0).astype(bf16)
        acc_ref[...] = jnp.zeros(acc_ref.shape, f32)
        m_ref[...] = jnp.full(m_ref.shape, NEG_INF, f32)

    def run(tiles):
        work = [(sub, h, masked) for sub, masked in tiles for h in range(FOX_HEADS)]
        causal = (lax.broadcasted_iota(jnp.int32, (T, T), 0)
                  <= lax.broadcasted_iota(jnp.int32, (T, T), 1))

        def scores(sub, h, masked, slot):
            keys = slice(sub * T, (sub + 1) * T)
            lhs = jnp.concatenate([k_ref[0, h // 2, keys, :], kb_ref[0, keys, :]], axis=1)
            st = _dot(lhs, qa_ref[h])
            if masked:
                st = jnp.where(causal, st, NEG_INF)
            s_ref[slot] = st
            return jnp.max(st, axis=0, keepdims=True)

        def update(sub, h, slot, bmax):
            cr = cref_ref[0, sub, h:h + 1, :]
            m_old = m_ref[h:h + 1, :]
            m_new = jnp.maximum(m_old, bmax - cr)
            alpha = jnp.exp2(m_old - m_new)
            p = jnp.exp2(s_ref[slot] - (m_new + cr)).astype(bf16)
            m_ref[h:h + 1, :] = m_new
            rows = slice(h * V_ROWS, (h + 1) * V_ROWS)
            pv = _dot(vt_ref[0, rows, sub * T:(sub + 1) * T], p)
            acc_ref[rows, :] = alpha * acc_ref[rows, :] + pv

        slots = s_ref.shape[0]
        bmax = {}
        for n in range(len(work) + ATTN_AHEAD):
            if n < len(work):
                sub, h, masked = work[n]
                bmax[n] = scores(sub, h, masked, n % slots)
            d = n - ATTN_AHEAD
            if d >= 0:
                sub, h, _ = work[d]
                update(sub, h, d % slots, bmax.pop(d))

    @pl.when(kind == 0)
    def _():
        run([(sub, False) for sub in range(KV_SUB)])

    @pl.when(kind == 1)
    def _():
        run([(0, True)])

    @pl.when(kind == 2)
    def _():
        run([(0, False), (1, True)])

    @pl.when(kind != 0)
    def _():
        for h in range(FOX_HEADS):
            r0 = h * V_ROWS
            o_ref[h * FOX_DH:(h + 1) * FOX_DH, :] = (
                acc_ref[r0:r0 + FOX_DH, :] / acc_ref[r0 + FOX_DH:r0 + FOX_DH + 1, :])
        o = o_ref[...].T
        y_b = (o * sg_ref[0].astype(f32)).astype(bf16)
        merged = pam_ref[0].astype(f32) + gb_ref[0].astype(f32) * _dot(y_b, wb_ref[...])
        out = _dot(merged.astype(bf16), wo_ref[...])
        y_ref[0] = x_ref[0] + _rms(out, gpost_ref[...])


def _fox_attn(qt, k, kb, vt, cref, sg, gb, pam, x, wb, wo, g_post):
    B, S, _ = x.shape
    T = min(TILE, S)
    nt = S // T
    assert nt % KV_SUB == 0
    steps = []
    for i in range(nt):
        for m in range(i // KV_SUB + 1):
            kind = 0 if m < i // KV_SUB else (1 if i % KV_SUB == 0 else 2)
            steps.append((i, m, kind))
    qi_tab, km_tab, kind_tab = (jnp.asarray([s[n] for s in steps], jnp.int32) for n in range(3))
    TK = KV_SUB * T
    full = lambda shape: pl.BlockSpec(shape, lambda b, s, qi, km, kind: (0,) * len(shape))
    qtile = pl.BlockSpec((1, T, D), lambda b, s, qi, km, kind: (b, qi[s], 0))
    grid_spec = pltpu.PrefetchScalarGridSpec(
        num_scalar_prefetch=3,
        grid=(B, len(steps)),
        in_specs=[pl.BlockSpec((1, D, T), lambda b, s, qi, km, kind: (b, 0, qi[s])),
                  pl.BlockSpec((1, D // LANES, TK, LANES),
                               lambda b, s, qi, km, kind: (b, 0, km[s], 0)),
                  pl.BlockSpec((1, TK, LANES), lambda b, s, qi, km, kind: (b, km[s], 0)),
                  pl.BlockSpec((1, FOX_HEADS * V_ROWS, TK),
                               lambda b, s, qi, km, kind: (b, 0, km[s])),
                  pl.BlockSpec((1, KV_SUB, FOX_HEADS, T),
                               lambda b, s, qi, km, kind: (b, km[s], 0, 0)),
                  qtile, qtile, qtile, qtile, full((D, D)), full((D, D)), full((1, D))],
        out_specs=qtile,
        scratch_shapes=[pltpu.VMEM((FOX_HEADS, 4 * FOX_DH, T), bf16),
                        pltpu.VMEM((FOX_HEADS * V_ROWS, T), f32),
                        pltpu.VMEM((FOX_HEADS, T), f32),
                        pltpu.VMEM((ATTN_AHEAD + 1, T, T), f32),
                        pltpu.VMEM((D, T), f32)],
    )
    return pl.pallas_call(
        _fox_attn_kernel,
        grid_spec=grid_spec,
        out_shape=jax.ShapeDtypeStruct((B, S, D), f32),
        compiler_params=pltpu.CompilerParams(
            dimension_semantics=("arbitrary", "arbitrary"), vmem_limit_bytes=VMEM_LIMIT),
        name="fox_attn",
    )(qi_tab, km_tab, kind_tab, qt, k, kb, vt, cref, sg, gb, pam, x, wb, wo, g_post)


def _block_diag_groups(w):
    nb, bw, _ = w.shape
    per = nb // LRU_GROUPS
    eye = jnp.eye(per, dtype=w.dtype)
    blocks = w.reshape(LRU_GROUPS, per, bw, 1, bw) * eye[None, :, None, :, None]
    return blocks.reshape(LRU_GROUPS, per * bw, per * bw)


def kernel(x, mem, g_pre, w_in, conv_w, conv_b, w_lru_r, b_lru_r, w_lru_i, b_lru_i, lru_lambda,
           b_forget, g_mem, w_mem_k, w_mem_v, w_branch, b_merge, w_out, g_post):
    row = lambda v: v.reshape(1, -1).astype(f32)
    sizes = (D, D, D, D, D, FOX_HEADS, D, D, D, 3 * D)
    offs = np.concatenate([[0], np.cumsum(sizes)])
    cols = [w_in[:, offs[n]:offs[n + 1]] for n in range(len(sizes))]
    w_ax, w_ag, w_fq, w_fk, w_fv, w_ff, w_fg, w_mq, w_mg, w_merge = cols
    w_mga, w_mgb, w_mgc = (w_merge[:, n * D:(n + 1) * D].astype(bf16) for n in range(3))
    w_ff = jnp.pad(w_ff, ((0, 0), (0, LANES - FOX_HEADS))).astype(bf16)
    b_f = jnp.pad(b_forget, (0, LANES - FOX_HEADS)).reshape(1, LANES).astype(f32)
    wb = w_branch.astype(bf16)
    g_pre_r = row(g_pre)

    mk, mv = _mem_kv(mem, row(g_mem), w_mem_k.astype(bf16), w_mem_v.astype(bf16))
    pa = _lru(x, g_pre_r, w_ax.astype(bf16), w_ag.astype(bf16), w_mga, conv_w.astype(f32),
              row(conv_b), _block_diag_groups(w_lru_r).astype(bf16),
              _block_diag_groups(w_lru_i).astype(bf16), row(b_lru_r), row(b_lru_i),
              row(lru_lambda), wb[0], row(b_merge[0]))
    pam = _memattn(x, g_pre_r, w_mq.astype(bf16), w_mg.astype(bf16), w_mgc, mk, mv, wb[2],
                   row(b_merge[2]), pa)
    qt, k, kb, vt, sg, gb, cref = _fox_proj(
        x, g_pre_r, w_fq.T.astype(bf16), w_fk.astype(bf16), w_fv.T.astype(bf16), w_ff, b_f,
        w_fg.astype(bf16), w_mgb, row(b_merge[1]))
    return _fox_attn(qt, k, kb, vt, cref, sg, gb, pam, x, wb[1], w_out.astype(bf16), row(g_post))
```

```python
import functools
import math

import numpy as np
import jax
import jax.numpy as jnp
from jax import lax
from jax.experimental import pallas as pl
from jax.experimental.pallas import tpu as pltpu

D = 1024
TILE = 512
FOX_HEADS = 16
FOX_DH = 64
MEM_HEADS = 4
MEM_DH = 256
LRU_GROUPS = 4
LRU_GW = D // LRU_GROUPS
LRU_C = 8.0
LRU_CHUNKS = 2
CONV_W = 4
RMS_EPS = 1e-6
NEG_INF = -1e30
LOG2E = math.log2(math.e)
V_ROWS = FOX_DH + 16
ATTN_AHEAD = 2
KV_SUB = 2
LANES = 128
SUBLANES = 8
VMEM_LIMIT = 56 * 1024 * 1024

f32 = jnp.float32
bf16 = jnp.bfloat16


def _dot(a, b):
    return jnp.dot(a, b, preferred_element_type=f32)


def _dot_nt(a, b):
    return lax.dot_general(a, b, (((1,), (1,)), ((), ())), preferred_element_type=f32)


def _rms(x, g):
    return x * lax.rsqrt(jnp.mean(x * x, axis=-1, keepdims=True) + RMS_EPS) * g


def _sigmoid(z):
    return 1.0 / (1.0 + jnp.exp2(z * -LOG2E))


def _softplus(z):
    return jnp.maximum(z, 0.0) + jnp.log1p(jnp.exp(-jnp.abs(z)))


def _split3(v):
    hi = v.astype(bf16)
    r1 = v - hi.astype(f32)
    mid = r1.astype(bf16)
    lo = (r1 - mid.astype(f32)).astype(bf16)
    return hi, mid, lo


def _mem_kv_kernel(mem_ref, g_ref, wk_ref, wv_ref, mk_ref, mv_ref):
    mn = _rms(mem_ref[0], g_ref[...]).astype(bf16)
    mk_ref[0] = _dot(mn, wk_ref[...]).astype(bf16)
    mv_ref[0] = _dot(mn, wv_ref[...]).astype(bf16)


def _mem_kv(mem, g_mem, wk, wv):
    B, M, _ = mem.shape
    full = lambda shape: pl.BlockSpec(shape, lambda b: (0,) * len(shape))
    return pl.pallas_call(
        _mem_kv_kernel,
        grid=(B,),
        in_specs=[pl.BlockSpec((1, M, D), lambda b: (b, 0, 0)), full((1, D)),
                  full((D, D)), full((D, D))],
        out_specs=[pl.BlockSpec((1, M, D), lambda b: (b, 0, 0))] * 2,
        out_shape=[jax.ShapeDtypeStruct((B, M, D), bf16)] * 2,
        compiler_params=pltpu.CompilerParams(
            dimension_semantics=("arbitrary",), vmem_limit_bytes=VMEM_LIMIT),
        name="mem_kv",
    )(mem, g_mem, wk, wv)


def _lru_kernel(x_ref, g_ref, wax_ref, wag_ref, wmg_ref, cw_ref, cb_ref, wr_ref, wi_ref,
                br_ref, bi_ref, lam_ref, wb_ref, bm_ref, pa_ref,
                xn_ref, xpad_ref, h_ref, hc_ref):
    NB, _, TT, _ = x_ref.shape
    R = NB * TT
    halo = (CONV_W - 1) * NB
    t = pl.program_id(1)

    @pl.when(t == 0)
    def _():
        xpad_ref[0:halo, :] = jnp.zeros((halo, D), f32)
        hc_ref[...] = jnp.zeros((NB, D), f32)

    x = jnp.swapaxes(x_ref[:, 0], 0, 1).reshape(R, D)
    xn_ref[...] = _rms(x, g_ref[...]).astype(bf16)

    RC = R // LRU_CHUNKS
    TC = TT // LRU_CHUNKS
    softplus_lam = _softplus(-lam_ref[...])

    def rows_of(c):
        return slice(c * RC, (c + 1) * RC)

    def conv_in(c):
        ax = _dot(xn_ref[rows_of(c), :], wax_ref[...])
        xpad_ref[halo + c * RC:halo + (c + 1) * RC, :] = ax

    def gates(c):
        xc = cb_ref[...] + cw_ref[0:1, :] * xpad_ref[c * RC:(c + 1) * RC, :]
        for k in range(1, CONV_W):
            xc = xc + cw_ref[k:k + 1, :] * xpad_ref[c * RC + k * NB:(c + 1) * RC + k * NB, :]
        xcb = xc.astype(bf16)
        r = jnp.concatenate(
            [_dot(xcb[:, g * LRU_GW:(g + 1) * LRU_GW], wr_ref[g]) for g in range(LRU_GROUPS)],
            axis=1)
        i = jnp.concatenate(
            [_dot(xcb[:, g * LRU_GW:(g + 1) * LRU_GW], wi_ref[g]) for g in range(LRU_GROUPS)],
            axis=1)
        return xc, r, i

    def recurrence(c, xc, r, i, h):
        r = _sigmoid(r + br_ref[...])
        i = _sigmoid(i + bi_ref[...])
        log_a = (-LRU_C * r) * softplus_lam
        a = jnp.exp(log_a)
        z = -jnp.tanh(log_a) * (a * a + 1.0)
        u = jnp.where(z > 0.0, z * lax.rsqrt(z), 0.0) * (i * xc)
        for s in range(TC):
            rows = slice(s * NB, (s + 1) * NB)
            h = a[rows, :] * h + u[rows, :]
            h_ref[c * RC + s * NB:c * RC + (s + 1) * NB, :] = h
        return h

    def project(c, ag, mg):
        y_a = (h_ref[rows_of(c), :] * (ag * _sigmoid(ag))).astype(bf16)
        pa = _sigmoid(mg + bm_ref[...]) * _dot(y_a, wb_ref[...])
        pa_ref[:, 0, c * TC:(c + 1) * TC, :] = (
            jnp.swapaxes(pa.reshape(TC, NB, D), 0, 1).astype(bf16))

    chunks = range(LRU_CHUNKS)
    for c in chunks:
        conv_in(c)
    staged = []
    for c in chunks:
        ag = _dot(xn_ref[rows_of(c), :], wag_ref[...])
        staged.append((ag,) + gates(c))
    xpad_ref[0:halo, :] = xpad_ref[R:R + halo, :]
    mgs = [_dot(xn_ref[rows_of(c), :], wmg_ref[...]) for c in chunks]
    h = hc_ref[...]
    for c in chunks:
        ag, xc, r, i = staged[c]
        h = recurrence(c, xc, r, i, h)
        project(c, ag, mgs[c])
    hc_ref[...] = h


def _lru(x, g_pre, wax, wag, wmg, conv_w, conv_b, wr, wi, br, bi, lam, wb, bm):
    B, S, _ = x.shape
    NB = SUBLANES
    assert B % NB == 0
    TT = min(TILE // NB, S)
    full = lambda shape: pl.BlockSpec(shape, lambda g, t: (0,) * len(shape))
    tile = pl.BlockSpec((NB, 1, TT, D), lambda g, t: (g, t, 0, 0))
    pa = pl.pallas_call(
        _lru_kernel,
        grid=(B // NB, S // TT),
        in_specs=[tile, full((1, D)), full((D, D)), full((D, D)), full((D, D)),
                  full((CONV_W, D)), full((1, D)),
                  full((LRU_GROUPS, LRU_GW, LRU_GW)), full((LRU_GROUPS, LRU_GW, LRU_GW)),
                  full((1, D)), full((1, D)), full((1, D)), full((D, D)), full((1, D))],
        out_specs=tile,
        out_shape=jax.ShapeDtypeStruct((B, S // TT, TT, D), bf16),
        scratch_shapes=[pltpu.VMEM((TT * NB, D), bf16),
                        pltpu.VMEM(((TT + CONV_W - 1) * NB, D), f32), pltpu.VMEM((TT * NB, D), f32),
                        pltpu.VMEM((NB, D), f32)],
        compiler_params=pltpu.CompilerParams(
            dimension_semantics=("arbitrary", "arbitrary"), vmem_limit_bytes=VMEM_LIMIT),
        name="lru",
    )(x.reshape(B, S // TT, TT, D), g_pre, wax, wag, wmg, conv_w, conv_b, wr, wi, br, bi, lam,
      wb, bm)
    return pa.reshape(B, S, D)


def _memattn_kernel(x_ref, g_ref, wq_ref, wg_ref, wmg_ref, mk_ref, mv_ref, wb_ref, bm_ref,
                    pa_ref, out_ref):
    xn = _rms(x_ref[0], g_ref[...]).astype(bf16)
    q = (_dot(xn, wq_ref[...]) * (1.0 / math.sqrt(MEM_DH))).astype(bf16)
    outs = []
    for h in range(MEM_HEADS):
        sl = slice(h * MEM_DH, (h + 1) * MEM_DH)
        s = _dot_nt(q[:, sl], mk_ref[0, :, sl])
        p = jnp.exp(s - jnp.max(s, axis=-1, keepdims=True))
        l = jnp.sum(p, axis=-1, keepdims=True)
        outs.append(_dot(p.astype(bf16), mv_ref[0, :, sl]) / l)
    o = jnp.concatenate(outs, axis=1)
    mg = _dot(xn, wg_ref[...])
    y_m = (o * (mg * _sigmoid(mg))).astype(bf16)
    gate = _sigmoid(_dot(xn, wmg_ref[...]) + bm_ref[...])
    out_ref[0] = (pa_ref[0].astype(f32) + gate * _dot(y_m, wb_ref[...])).astype(bf16)


def _memattn(x, g_pre, wq, wg, wmg, mk, mv, wb, bm, pa):
    B, S, _ = x.shape
    M = mk.shape[1]
    T = min(TILE, S)
    full = lambda shape: pl.BlockSpec(shape, lambda b, t: (0,) * len(shape))
    tile = pl.BlockSpec((1, T, D), lambda b, t: (b, t, 0))
    memb = pl.BlockSpec((1, M, D), lambda b, t: (b, 0, 0))
    return pl.pallas_call(
        _memattn_kernel,
        grid=(B, S // T),
        in_specs=[tile, full((1, D)), full((D, D)), full((D, D)), full((D, D)), memb, memb,
                  full((D, D)), full((1, D)), tile],
        out_specs=tile,
        out_shape=jax.ShapeDtypeStruct((B, S, D), bf16),
        compiler_params=pltpu.CompilerParams(
            dimension_semantics=("arbitrary", "arbitrary"), vmem_limit_bytes=VMEM_LIMIT),
        name="memattn",
    )(x, g_pre, wq, wg, wmg, mk, mv, wb, bm, pa)


def _fox_proj_kernel(x_ref, g_ref, wqt_ref, wk_ref, wvt_ref, wf_ref, bf_ref, wfg_ref, wmg_ref,
                     bm_ref,
                     qt_ref, k_ref, kb_ref, vt_ref, sg_ref, gb_ref, cref_ref, carry_ref, d_ref):
    T = x_ref.shape[1]
    t = pl.program_id(1)

    @pl.when(t == 0)
    def _():
        carry_ref[...] = jnp.zeros((SUBLANES, LANES), f32)

    xn = _rms(x_ref[0], g_ref[...]).astype(bf16)

    lane = lax.broadcasted_iota(jnp.int32, (T, LANES), 1)
    z = _dot(xn, wf_ref[...]) + bf_ref[...]
    qt_ref[0] = (_dot_nt(wqt_ref[...], xn) * (LOG2E / math.sqrt(FOX_DH))).astype(bf16)
    log_f = jnp.where(lane < FOX_HEADS, -LOG2E * _softplus(-z), 0.0)
    row = lax.broadcasted_iota(jnp.int32, (T, LANES), 0) % SUBLANES
    scan = log_f
    shift = 1
    while shift < SUBLANES:
        scan = scan + jnp.where(row >= shift, pltpu.roll(scan, shift, 0), 0.0)
        shift *= 2
    d_ref[...] = scan
    run = jnp.zeros((SUBLANES, LANES), f32)
    for grp in range(T // SUBLANES):
        rows = slice(grp * SUBLANES, (grp + 1) * SUBLANES)
        blk = d_ref[rows, :] + run
        d_ref[rows, :] = blk
        run = jnp.broadcast_to(blk[SUBLANES - 1:SUBLANES, :], (SUBLANES, LANES))
    d = d_ref[...]
    vt = _dot_nt(wvt_ref[...], xn).astype(bf16)
    ones = jnp.ones((V_ROWS - FOX_DH, T), bf16)
    for h in range(FOX_HEADS):
        vt_ref[0, h * V_ROWS:h * V_ROWS + FOX_DH, :] = vt[h * FOX_DH:(h + 1) * FOX_DH, :]
        vt_ref[0, h * V_ROWS + FOX_DH:(h + 1) * V_ROWS, :] = ones
    kb_ref[0] = -d
    kz = _dot(xn, wk_ref[...]).astype(bf16)
    for h in range(FOX_HEADS):
        k_ref[0, h] = kz[:, h * FOX_DH:(h + 1) * FOX_DH]

    carry = carry_ref[0:1, :]
    ident = (lax.broadcasted_iota(jnp.int32, (FOX_HEADS, LANES), 0)
             == lax.broadcasted_iota(jnp.int32, (FOX_HEADS, LANES), 1))
    col = jnp.sum(jnp.where(ident, jnp.broadcast_to(carry, (FOX_HEADS, LANES)), 0.0),
                  axis=1, keepdims=True)
    cref_ref[0, 0] = jnp.broadcast_to(col, (FOX_HEADS, T))
    carry_ref[...] = jnp.broadcast_to(carry + d[T - 1:T, :], (SUBLANES, LANES))

    fg = _dot(xn, wfg_ref[...])
    sg_ref[0] = (fg * _sigmoid(fg)).astype(bf16)
    gb_ref[0] = _sigmoid(_dot(xn, wmg_ref[...]) + bm_ref[...]).astype(bf16)


def _fox_proj(x, g_pre, wqt, wk, wvt, wf, bf, wfg, wmg, bm):
    B, S, _ = x.shape
    T = min(TILE, S)
    nt = S // T
    full = lambda shape: pl.BlockSpec(shape, lambda b, t: (0,) * len(shape))
    tile = pl.BlockSpec((1, T, D), lambda b, t: (b, t, 0))
    tile_t = pl.BlockSpec((1, D, T), lambda b, t: (b, 0, t))
    vrows = FOX_HEADS * V_ROWS
    return pl.pallas_call(
        _fox_proj_kernel,
        grid=(B, nt),
        in_specs=[tile, full((1, D)), full((D, D)), full((D, D)), full((D, D)),
                  full((D, LANES)), full((1, LANES)), full((D, D)), full((D, D)), full((1, D))],
        out_specs=[tile_t,
                   pl.BlockSpec((1, FOX_HEADS, T, FOX_DH), lambda b, t: (b, 0, t, 0)),
                   pl.BlockSpec((1, T, LANES), lambda b, t: (b, t, 0)),
                   pl.BlockSpec((1, vrows, T), lambda b, t: (b, 0, t)), tile, tile,
                   pl.BlockSpec((1, 1, FOX_HEADS, T), lambda b, t: (b, t, 0, 0))],
        out_shape=[jax.ShapeDtypeStruct((B, D, S), bf16),
                   jax.ShapeDtypeStruct((B, FOX_HEADS, S, FOX_DH), bf16),
                   jax.ShapeDtypeStruct((B, S, LANES), f32),
                   jax.ShapeDtypeStruct((B, vrows, S), bf16),
                   jax.ShapeDtypeStruct((B, S, D), bf16),
                   jax.ShapeDtypeStruct((B, S, D), bf16),
                   jax.ShapeDtypeStruct((B, nt, FOX_HEADS, T), f32)],
        scratch_shapes=[pltpu.VMEM((SUBLANES, LANES), f32), pltpu.VMEM((T, LANES), f32)],
        compiler_params=pltpu.CompilerParams(
            dimension_semantics=("arbitrary", "arbitrary"), vmem_limit_bytes=VMEM_LIMIT),
        name="fox_proj",
    )(x, g_pre, wqt, wk, wvt, wf, bf, wfg, wmg, bm)


def _fox_attn_kernel(qi_ref, km_ref, kind_ref, qt_ref, k_ref, kb_ref, vt_ref, cref_ref, sg_ref,
                     gb_ref, pam_ref, x_ref, wb_ref, wo_ref, gpost_ref, y_ref,
                     acc_ref, m_ref, s_ref, o_ref):
    T = x_ref.shape[1]
    step = pl.program_id(1)
    kind = kind_ref[step]

    @pl.when(km_ref[step] == 0)
    def _():
        acc_ref[...] = jnp.zeros(acc_ref.shape, f32)
        m_ref[...] = jnp.full(m_ref.shape, NEG_INF, f32)

    def run(tiles):
        work = [(sub, h, masked) for sub, masked in tiles for h in range(FOX_HEADS)]
        causal = (lax.broadcasted_iota(jnp.int32, (T, T), 0)
                  <= lax.broadcasted_iota(jnp.int32, (T, T), 1))

        def scores(sub, h, masked, slot):
            keys = slice(sub * T, (sub + 1) * T)
            q_h = qt_ref[0, h * FOX_DH:(h + 1) * FOX_DH, :]
            st = _dot(k_ref[0, h, keys, :], q_h)
            st = st + jnp.broadcast_to(kb_ref[0, keys, h:h + 1], (T, T))
            if masked:
                st = jnp.where(causal, st, NEG_INF)
            s_ref[slot] = st
            return jnp.max(st, axis=0, keepdims=True)

        def update(sub, h, slot, bmax):
            cr = cref_ref[0, sub, h:h + 1, :]
            m_old = m_ref[h:h + 1, :]
            m_new = jnp.maximum(m_old, bmax - cr)
            alpha = jnp.exp2(m_old - m_new)
            p = jnp.exp2(s_ref[slot] - (m_new + cr)).astype(bf16)
            m_ref[h:h + 1, :] = m_new
            rows = slice(h * V_ROWS, (h + 1) * V_ROWS)
            pv = _dot(vt_ref[0, rows, sub * T:(sub + 1) * T], p)
            acc_ref[rows, :] = alpha * acc_ref[rows, :] + pv

        slots = s_ref.shape[0]
        bmax = {}
        for n in range(len(work) + ATTN_AHEAD):
            if n < len(work):
                sub, h, masked = work[n]
                bmax[n] = scores(sub, h, masked, n % slots)
            d = n - ATTN_AHEAD
            if d >= 0:
                sub, h, _ = work[d]
                update(sub, h, d % slots, bmax.pop(d))

    @pl.when(kind == 0)
    def _():
        run([(sub, False) for sub in range(KV_SUB)])

    @pl.when(kind == 1)
    def _():
        run([(0, True)])

    @pl.when(kind == 2)
    def _():
        run([(0, False), (1, True)])

    @pl.when(kind != 0)
    def _():
        for h in range(FOX_HEADS):
            r0 = h * V_ROWS
            o_ref[h * FOX_DH:(h + 1) * FOX_DH, :] = (
                acc_ref[r0:r0 + FOX_DH, :] / acc_ref[r0 + FOX_DH:r0 + FOX_DH + 1, :])
        o = o_ref[...].T
        y_b = (o * sg_ref[0].astype(f32)).astype(bf16)
        merged = pam_ref[0].astype(f32) + gb_ref[0].astype(f32) * _dot(y_b, wb_ref[...])
        out = _dot(merged.astype(bf16), wo_ref[...])
        y_ref[0] = x_ref[0] + _rms(out, gpost_ref[...])


def _fox_attn(qt, k, kb, vt, cref, sg, gb, pam, x, wb, wo, g_post):
    B, S, _ = x.shape
    T = min(TILE, S)
    nt = S // T
    assert nt % KV_SUB == 0
    steps = []
    for i in range(nt):
        for m in range(i // KV_SUB + 1):
            kind = 0 if m < i // KV_SUB else (1 if i % KV_SUB == 0 else 2)
            steps.append((i, m, kind))
    qi_tab, km_tab, kind_tab = (jnp.asarray([s[n] for s in steps], jnp.int32) for n in range(3))
    TK = KV_SUB * T
    full = lambda shape: pl.BlockSpec(shape, lambda b, s, qi, km, kind: (0,) * len(shape))
    qtile = pl.BlockSpec((1, T, D), lambda b, s, qi, km, kind: (b, qi[s], 0))
    grid_spec = pltpu.PrefetchScalarGridSpec(
        num_scalar_prefetch=3,
        grid=(B, len(steps)),
        in_specs=[pl.BlockSpec((1, D, T), lambda b, s, qi, km, kind: (b, 0, qi[s])),
                  pl.BlockSpec((1, FOX_HEADS, TK, FOX_DH),
                               lambda b, s, qi, km, kind: (b, 0, km[s], 0)),
                  pl.BlockSpec((1, TK, LANES), lambda b, s, qi, km, kind: (b, km[s], 0)),
                  pl.BlockSpec((1, FOX_HEADS * V_ROWS, TK),
                               lambda b, s, qi, km, kind: (b, 0, km[s])),
                  pl.BlockSpec((1, KV_SUB, FOX_HEADS, T),
                               lambda b, s, qi, km, kind: (b, km[s], 0, 0)),
                  qtile, qtile, qtile, qtile, full((D, D)), full((D, D)), full((1, D))],
        out_specs=qtile,
        scratch_shapes=[pltpu.VMEM((FOX_HEADS * V_ROWS, T), f32),
                        pltpu.VMEM((FOX_HEADS, T), f32),
                        pltpu.VMEM((ATTN_AHEAD + 1, T, T), f32),
                        pltpu.VMEM((D, T), f32)],
    )
    return pl.pallas_call(
        _fox_attn_kernel,
        grid_spec=grid_spec,
        out_shape=jax.ShapeDtypeStruct((B, S, D), f32),
        compiler_params=pltpu.CompilerParams(
            dimension_semantics=("arbitrary", "arbitrary"), vmem_limit_bytes=VMEM_LIMIT),
        name="fox_attn",
    )(qi_tab, km_tab, kind_tab, qt, k, kb, vt, cref, sg, gb, pam, x, wb, wo, g_post)


def _block_diag_groups(w):
    nb, bw, _ = w.shape
    per = nb // LRU_GROUPS
    eye = jnp.eye(per, dtype=w.dtype)
    blocks = w.reshape(LRU_GROUPS, per, bw, 1, bw) * eye[None, :, None, :, None]
    return blocks.reshape(LRU_GROUPS, per * bw, per * bw)


def kernel(x, mem, g_pre, w_in, conv_w, conv_b, w_lru_r, b_lru_r, w_lru_i, b_lru_i, lru_lambda,
           b_forget, g_mem, w_mem_k, w_mem_v, w_branch, b_merge, w_out, g_post):
    row = lambda v: v.reshape(1, -1).astype(f32)
    sizes = (D, D, D, D, D, FOX_HEADS, D, D, D, 3 * D)
    offs = np.concatenate([[0], np.cumsum(sizes)])
    cols = [w_in[:, offs[n]:offs[n + 1]] for n in range(len(sizes))]
    w_ax, w_ag, w_fq, w_fk, w_fv, w_ff, w_fg, w_mq, w_mg, w_merge = cols
    w_mga, w_mgb, w_mgc = (w_merge[:, n * D:(n + 1) * D].astype(bf16) for n in range(3))
    w_ff = jnp.pad(w_ff, ((0, 0), (0, LANES - FOX_HEADS))).astype(bf16)
    b_f = jnp.pad(b_forget, (0, LANES - FOX_HEADS)).reshape(1, LANES).astype(f32)
    wb = w_branch.astype(bf16)
    g_pre_r = row(g_pre)

    mk, mv = _mem_kv(mem, row(g_mem), w_mem_k.astype(bf16), w_mem_v.astype(bf16))
    pa = _lru(x, g_pre_r, w_ax.astype(bf16), w_ag.astype(bf16), w_mga, conv_w.astype(f32),
              row(conv_b), _block_diag_groups(w_lru_r).astype(bf16),
              _block_diag_groups(w_lru_i).astype(bf16), row(b_lru_r), row(b_lru_i),
              row(lru_lambda), wb[0], row(b_merge[0]))
    pam = _memattn(x, g_pre_r, w_mq.astype(bf16), w_mg.astype(bf16), w_mgc, mk, mv, wb[2],
                   row(b_merge[2]), pa)
    qt, k, kb, vt, sg, gb, cref = _fox_proj(
        x, g_pre_r, w_fq.T.astype(bf16), w_fk.astype(bf16), w_fv.T.astype(bf16), w_ff, b_f,
        w_fg.astype(bf16), w_mgb, row(b_merge[1]))
    return _fox_attn(qt, k, kb, vt, cref, sg, gb, pam, x, wb[1], w_out.astype(bf16), row(g_post))
```

```python
import functools
import math

import numpy as np
import jax
import jax.numpy as jnp
from jax import lax
from jax.experimental import pallas as pl
from jax.experimental.pallas import tpu as pltpu

D = 1024
TILE = 512
FOX_HEADS = 16
FOX_DH = 64
MEM_HEADS = 4
MEM_DH = 256
LRU_GROUPS = 4
LRU_GW = D // LRU_GROUPS
LRU_C = 8.0
LRU_CHUNKS = 2
CONV_W = 4
RMS_EPS = 1e-6
NEG_INF = -1e30
LOG2E = math.log2(math.e)
V_ROWS = FOX_DH + 16
ATTN_AHEAD = 2
KV_SUB = 2
LANES = 128
SUBLANES = 8
VMEM_LIMIT = 56 * 1024 * 1024

f32 = jnp.float32
bf16 = jnp.bfloat16


def _dot(a, b):
    return jnp.dot(a, b, preferred_element_type=f32)


def _dot_nt(a, b):
    return lax.dot_general(a, b, (((1,), (1,)), ((), ())), preferred_element_type=f32)


def _rms(x, g):
    return x * lax.rsqrt(jnp.mean(x * x, axis=-1, keepdims=True) + RMS_EPS) * g


def _sigmoid(z):
    return 1.0 / (1.0 + jnp.exp2(z * -LOG2E))


def _softplus(z):
    return jnp.maximum(z, 0.0) + jnp.log1p(jnp.exp(-jnp.abs(z)))


def _split3(v):
    hi = v.astype(bf16)
    r1 = v - hi.astype(f32)
    mid = r1.astype(bf16)
    lo = (r1 - mid.astype(f32)).astype(bf16)
    return hi, mid, lo


def _mem_kv_kernel(mem_ref, g_ref, wk_ref, wv_ref, mk_ref, mv_ref):
    mn = _rms(mem_ref[0], g_ref[...]).astype(bf16)
    mk_ref[0] = _dot(mn, wk_ref[...]).astype(bf16)
    mv_ref[0] = _dot(mn, wv_ref[...]).astype(bf16)


def _mem_kv(mem, g_mem, wk, wv):
    B, M, _ = mem.shape
    full = lambda shape: pl.BlockSpec(shape, lambda b: (0,) * len(shape))
    return pl.pallas_call(
        _mem_kv_kernel,
        grid=(B,),
        in_specs=[pl.BlockSpec((1, M, D), lambda b: (b, 0, 0)), full((1, D)),
                  full((D, D)), full((D, D))],
        out_specs=[pl.BlockSpec((1, M, D), lambda b: (b, 0, 0))] * 2,
        out_shape=[jax.ShapeDtypeStruct((B, M, D), bf16)] * 2,
        compiler_params=pltpu.CompilerParams(
            dimension_semantics=("arbitrary",), vmem_limit_bytes=VMEM_LIMIT),
        name="mem_kv",
    )(mem, g_mem, wk, wv)


def _lru_kernel(x_ref, g_ref, wax_ref, wag_ref, wmg_ref, cw_ref, cb_ref, wr_ref, wi_ref,
                br_ref, bi_ref, lam_ref, wb_ref, bm_ref, pa_ref,
                xn_ref, xpad_ref, h_ref, hc_ref):
    NB, _, TT, _ = x_ref.shape
    R = NB * TT
    halo = (CONV_W - 1) * NB
    t = pl.program_id(1)

    @pl.when(t == 0)
    def _():
        xpad_ref[0:halo, :] = jnp.zeros((halo, D), f32)
        hc_ref[...] = jnp.zeros((NB, D), f32)

    x = jnp.swapaxes(x_ref[:, 0], 0, 1).reshape(R, D)
    xn_ref[...] = _rms(x, g_ref[...]).astype(bf16)

    RC = R // LRU_CHUNKS
    TC = TT // LRU_CHUNKS
    softplus_lam = _softplus(-lam_ref[...])

    def rows_of(c):
        return slice(c * RC, (c + 1) * RC)

    def conv_in(c):
        ax = _dot(xn_ref[rows_of(c), :], wax_ref[...])
        xpad_ref[halo + c * RC:halo + (c + 1) * RC, :] = ax

    def gates(c):
        xc = cb_ref[...] + cw_ref[0:1, :] * xpad_ref[c * RC:(c + 1) * RC, :]
        for k in range(1, CONV_W):
            xc = xc + cw_ref[k:k + 1, :] * xpad_ref[c * RC + k * NB:(c + 1) * RC + k * NB, :]
        xcb = xc.astype(bf16)
        r = jnp.concatenate(
            [_dot(xcb[:, g * LRU_GW:(g + 1) * LRU_GW], wr_ref[g]) for g in range(LRU_GROUPS)],
            axis=1)
        i = jnp.concatenate(
            [_dot(xcb[:, g * LRU_GW:(g + 1) * LRU_GW], wi_ref[g]) for g in range(LRU_GROUPS)],
            axis=1)
        return xc, r, i

    def recurrence(c, xc, r, i, h):
        r = _sigmoid(r + br_ref[...])
        i = _sigmoid(i + bi_ref[...])
        log_a = (-LRU_C * r) * softplus_lam
        a = jnp.exp(log_a)
        z = -jnp.tanh(log_a) * (a * a + 1.0)
        u = jnp.where(z > 0.0, z * lax.rsqrt(z), 0.0) * (i * xc)
        for s in range(TC):
            rows = slice(s * NB, (s + 1) * NB)
            h = a[rows, :] * h + u[rows, :]
            h_ref[c * RC + s * NB:c * RC + (s + 1) * NB, :] = h
        return h

    def project(c, ag, mg):
        y_a = (h_ref[rows_of(c), :] * (ag * _sigmoid(ag))).astype(bf16)
        pa = _sigmoid(mg + bm_ref[...]) * _dot(y_a, wb_ref[...])
        pa_ref[:, 0, c * TC:(c + 1) * TC, :] = (
            jnp.swapaxes(pa.reshape(TC, NB, D), 0, 1).astype(bf16))

    chunks = range(LRU_CHUNKS)
    for c in chunks:
        conv_in(c)
    staged = []
    for c in chunks:
        ag = _dot(xn_ref[rows_of(c), :], wag_ref[...])
        staged.append((ag,) + gates(c))
    xpad_ref[0:halo, :] = xpad_ref[R:R + halo, :]
    mgs = [_dot(xn_ref[rows_of(c), :], wmg_ref[...]) for c in chunks]
    h = hc_ref[...]
    for c in chunks:
        ag, xc, r, i = staged[c]
        h = recurrence(c, xc, r, i, h)
        project(c, ag, mgs[c])
    hc_ref[...] = h


def _lru(x, g_pre, wax, wag, wmg, conv_w, conv_b, wr, wi, br, bi, lam, wb, bm):
    B, S, _ = x.shape
    NB = SUBLANES
    assert B % NB == 0
    TT = min(TILE // NB, S)
    full = lambda shape: pl.BlockSpec(shape, lambda g, t: (0,) * len(shape))
    tile = pl.BlockSpec((NB, 1, TT, D), lambda g, t: (g, t, 0, 0))
    pa = pl.pallas_call(
        _lru_kernel,
        grid=(B // NB, S // TT),
        in_specs=[tile, full((1, D)), full((D, D)), full((D, D)), full((D, D)),
                  full((CONV_W, D)), full((1, D)),
                  full((LRU_GROUPS, LRU_GW, LRU_GW)), full((LRU_GROUPS, LRU_GW, LRU_GW)),
                  full((1, D)), full((1, D)), full((1, D)), full((D, D)), full((1, D))],
        out_specs=tile,
        out_shape=jax.ShapeDtypeStruct((B, S // TT, TT, D), bf16),
        scratch_shapes=[pltpu.VMEM((TT * NB, D), bf16),
                        pltpu.VMEM(((TT + CONV_W - 1) * NB, D), f32), pltpu.VMEM((TT * NB, D), f32),
                        pltpu.VMEM((NB, D), f32)],
        compiler_params=pltpu.CompilerParams(
            dimension_semantics=("arbitrary", "arbitrary"), vmem_limit_bytes=VMEM_LIMIT),
        name="lru",
    )(x.reshape(B, S // TT, TT, D), g_pre, wax, wag, wmg, conv_w, conv_b, wr, wi, br, bi, lam,
      wb, bm)
    return pa.reshape(B, S, D)


def _memattn_kernel(x_ref, g_ref, wq_ref, wg_ref, wmg_ref, mk_ref, mv_ref, wb_ref, bm_ref,
                    pa_ref, out_ref):
    xn = _rms(x_ref[0], g_ref[...]).astype(bf16)
    q = (_dot(xn, wq_ref[...]) * (1.0 / math.sqrt(MEM_DH))).astype(bf16)
    outs = []
    for h in range(MEM_HEADS):
        sl = slice(h * MEM_DH, (h + 1) * MEM_DH)
        s = _dot_nt(q[:, sl], mk_ref[0, :, sl])
        p = jnp.exp(s - jnp.max(s, axis=-1, keepdims=True))
        l = jnp.sum(p, axis=-1, keepdims=True)
        outs.append(_dot(p.astype(bf16), mv_ref[0, :, sl]) / l)
    o = jnp.concatenate(outs, axis=1)
    mg = _dot(xn, wg_ref[...])
    y_m = (o * (mg * _sigmoid(mg))).astype(bf16)
    gate = _sigmoid(_dot(xn, wmg_ref[...]) + bm_ref[...])
    out_ref[0] = (pa_ref[0].astype(f32) + gate * _dot(y_m, wb_ref[...])).astype(bf16)


def _memattn(x, g_pre, wq, wg, wmg, mk, mv, wb, bm, pa):
    B, S, _ = x.shape
    M = mk.shape[1]
    T = min(TILE, S)
    full = lambda shape: pl.BlockSpec(shape, lambda b, t: (0,) * len(shape))
    tile = pl.BlockSpec((1, T, D), lambda b, t: (b, t, 0))
    memb = pl.BlockSpec((1, M, D), lambda b, t: (b, 0, 0))
    return pl.pallas_call(
        _memattn_kernel,
        grid=(B, S // T),
        in_specs=[tile, full((1, D)), full((D, D)), full((D, D)), full((D, D)), memb, memb,
                  full((D, D)), full((1, D)), tile],
        out_specs=tile,
        out_shape=jax.ShapeDtypeStruct((B, S, D), bf16),
        compiler_params=pltpu.CompilerParams(
            dimension_semantics=("arbitrary", "arbitrary"), vmem_limit_bytes=VMEM_LIMIT),
        name="memattn",
    )(x, g_pre, wq, wg, wmg, mk, mv, wb, bm, pa)


def _fox_proj_kernel(x_ref, g_ref, wqt_ref, wk_ref, wvt_ref, wf_ref, bf_ref, wfg_ref, wmg_ref,
                     bm_ref,
                     qt_ref, k_ref, vt_ref, sg_ref, gb_ref, cref_ref, carry_ref, d_ref):
    T = x_ref.shape[1]
    t = pl.program_id(1)

    @pl.when(t == 0)
    def _():
        carry_ref[...] = jnp.zeros((SUBLANES, LANES), f32)

    xn = _rms(x_ref[0], g_ref[...]).astype(bf16)

    lane = lax.broadcasted_iota(jnp.int32, (T, LANES), 1)
    z = _dot(xn, wf_ref[...]) + bf_ref[...]
    qt = (_dot_nt(wqt_ref[...], xn) * (LOG2E / math.sqrt(FOX_DH))).astype(bf16)
    ones = jnp.ones((V_ROWS - FOX_DH, T), bf16)
    for h in range(FOX_HEADS):
        qt_ref[0, h * V_ROWS:h * V_ROWS + FOX_DH, :] = qt[h * FOX_DH:(h + 1) * FOX_DH, :]
        qt_ref[0, h * V_ROWS + FOX_DH:(h + 1) * V_ROWS, :] = ones
    log_f = jnp.where(lane < FOX_HEADS, -LOG2E * _softplus(-z), 0.0)
    row = lax.broadcasted_iota(jnp.int32, (T, LANES), 0) % SUBLANES
    scan = log_f
    shift = 1
    while shift < SUBLANES:
        scan = scan + jnp.where(row >= shift, pltpu.roll(scan, shift, 0), 0.0)
        shift *= 2
    d_ref[...] = scan
    run = jnp.zeros((SUBLANES, LANES), f32)
    for grp in range(T // SUBLANES):
        rows = slice(grp * SUBLANES, (grp + 1) * SUBLANES)
        blk = d_ref[rows, :] + run
        d_ref[rows, :] = blk
        run = jnp.broadcast_to(blk[SUBLANES - 1:SUBLANES, :], (SUBLANES, LANES))
    d = d_ref[...]
    vt = _dot_nt(wvt_ref[...], xn).astype(bf16)
    for h in range(FOX_HEADS):
        vt_ref[0, h * V_ROWS:h * V_ROWS + FOX_DH, :] = vt[h * FOX_DH:(h + 1) * FOX_DH, :]
        vt_ref[0, h * V_ROWS + FOX_DH:(h + 1) * V_ROWS, :] = ones
    hi, mid, lo = (part.astype(f32) for part in _split3(-d))
    kz = _dot(xn, wk_ref[...])
    for h in range(FOX_HEADS):
        pair = kz[:, (h // 2) * LANES:(h // 2 + 1) * LANES]
        if h % 2:
            pair = pltpu.roll(pair, FOX_DH, 1)
        tile_h = jnp.where(lane < FOX_DH, pair, 0.0)
        for n, part in enumerate((hi, mid, lo)):
            tile_h = tile_h + jnp.where(lane == FOX_DH + n,
                                        pltpu.roll(part, (FOX_DH + n - h) % LANES, 1), 0.0)
        k_ref[0, h] = tile_h[:, 0:V_ROWS].astype(bf16)

    carry = carry_ref[0:1, :]
    ident = (lax.broadcasted_iota(jnp.int32, (FOX_HEADS, LANES), 0)
             == lax.broadcasted_iota(jnp.int32, (FOX_HEADS, LANES), 1))
    col = jnp.sum(jnp.where(ident, jnp.broadcast_to(carry, (FOX_HEADS, LANES)), 0.0),
                  axis=1, keepdims=True)
    cref_ref[0, 0] = jnp.broadcast_to(col, (FOX_HEADS, T))
    carry_ref[...] = jnp.broadcast_to(carry + d[T - 1:T, :], (SUBLANES, LANES))

    fg = _dot(xn, wfg_ref[...])
    sg_ref[0] = (fg * _sigmoid(fg)).astype(bf16)
    gb_ref[0] = _sigmoid(_dot(xn, wmg_ref[...]) + bm_ref[...]).astype(bf16)


def _fox_proj(x, g_pre, wqt, wk, wvt, wf, bf, wfg, wmg, bm):
    B, S, _ = x.shape
    T = min(TILE, S)
    nt = S // T
    full = lambda shape: pl.BlockSpec(shape, lambda b, t: (0,) * len(shape))
    tile = pl.BlockSpec((1, T, D), lambda b, t: (b, t, 0))
    vrows = FOX_HEADS * V_ROWS
    return pl.pallas_call(
        _fox_proj_kernel,
        grid=(B, nt),
        in_specs=[tile, full((1, D)), full((D, D)), full((D, D)), full((D, D)),
                  full((D, LANES)), full((1, LANES)), full((D, D)), full((D, D)), full((1, D))],
        out_specs=[pl.BlockSpec((1, vrows, T), lambda b, t: (b, 0, t)),
                   pl.BlockSpec((1, FOX_HEADS, T, V_ROWS), lambda b, t: (b, 0, t, 0)),
                   pl.BlockSpec((1, vrows, T), lambda b, t: (b, 0, t)), tile, tile,
                   pl.BlockSpec((1, 1, FOX_HEADS, T), lambda b, t: (b, t, 0, 0))],
        out_shape=[jax.ShapeDtypeStruct((B, vrows, S), bf16),
                   jax.ShapeDtypeStruct((B, FOX_HEADS, S, V_ROWS), bf16),
                   jax.ShapeDtypeStruct((B, vrows, S), bf16),
                   jax.ShapeDtypeStruct((B, S, D), bf16),
                   jax.ShapeDtypeStruct((B, S, D), bf16),
                   jax.ShapeDtypeStruct((B, nt, FOX_HEADS, T), f32)],
        scratch_shapes=[pltpu.VMEM((SUBLANES, LANES), f32), pltpu.VMEM((T, LANES), f32)],
        compiler_params=pltpu.CompilerParams(
            dimension_semantics=("arbitrary", "arbitrary"), vmem_limit_bytes=VMEM_LIMIT),
        name="fox_proj",
    )(x, g_pre, wqt, wk, wvt, wf, bf, wfg, wmg, bm)


def _fox_attn_kernel(qi_ref, km_ref, kind_ref, qt_ref, k_ref, vt_ref, cref_ref, sg_ref,
                     gb_ref, pam_ref, x_ref, wb_ref, wo_ref, gpost_ref, y_ref,
                     acc_ref, m_ref, s_ref, o_ref):
    T = x_ref.shape[1]
    step = pl.program_id(1)
    kind = kind_ref[step]

    @pl.when(km_ref[step] == 0)
    def _():
        acc_ref[...] = jnp.zeros(acc_ref.shape, f32)
        m_ref[...] = jnp.full(m_ref.shape, NEG_INF, f32)

    def run(tiles):
        work = [(sub, h, masked) for sub, masked in tiles for h in range(FOX_HEADS)]
        H = T // 2
        causal = (lax.broadcasted_iota(jnp.int32, (H, T), 0)
                  <= lax.broadcasted_iota(jnp.int32, (H, T), 1))

        def logits(h, keys, queries):
            return _dot(k_ref[0, h, keys, :], qt_ref[0, h * V_ROWS:(h + 1) * V_ROWS, queries])

        def scores(sub, h, masked, slot):
            k0 = sub * T
            if not masked:
                st = logits(h, slice(k0, k0 + T), slice(0, T))
                s_ref[slot] = st
                return jnp.max(st, axis=0, keepdims=True)
            top = jnp.where(causal, logits(h, slice(k0, k0 + H), slice(0, T)), NEG_INF)
            low = jnp.where(causal[:, 0:H], logits(h, slice(k0 + H, k0 + T), slice(H, T)), NEG_INF)
            s_ref[slot, 0:H, :] = top
            s_ref[slot, H:T, H:T] = low
            top_max = jnp.max(top, axis=0, keepdims=True)
            low_max = jnp.max(low, axis=0, keepdims=True)
            return jnp.concatenate(
                [top_max[:, 0:H], jnp.maximum(top_max[:, H:T], low_max)], axis=1)

        def update(sub, h, masked, slot, bmax):
            cr = cref_ref[0, sub, h:h + 1, :]
            m_old = m_ref[h:h + 1, :]
            m_new = jnp.maximum(m_old, bmax - cr)
            alpha = jnp.exp2(m_old - m_new)
            shift = m_new + cr
            m_ref[h:h + 1, :] = m_new
            rows = slice(h * V_ROWS, (h + 1) * V_ROWS)
            k0 = sub * T
            if not masked:
                p = jnp.exp2(s_ref[slot] - shift).astype(bf16)
                pv = _dot(vt_ref[0, rows, k0:k0 + T], p)
                acc_ref[rows, :] = alpha * acc_ref[rows, :] + pv
                return
            p_top = jnp.exp2(s_ref[slot, 0:H, :] - shift).astype(bf16)
            p_low = jnp.exp2(s_ref[slot, H:T, H:T] - shift[:, H:T]).astype(bf16)
            pv = _dot(vt_ref[0, rows, k0:k0 + H], p_top)
            pv_low = _dot(vt_ref[0, rows, k0 + H:k0 + T], p_low)
            acc_ref[rows, 0:H] = alpha[:, 0:H] * acc_ref[rows, 0:H] + pv[:, 0:H]
            acc_ref[rows, H:T] = alpha[:, H:T] * acc_ref[rows, H:T] + (pv[:, H:T] + pv_low)

        slots = s_ref.shape[0]
        bmax = {}
        for n in range(len(work) + ATTN_AHEAD):
            if n < len(work):
                sub, h, masked = work[n]
                bmax[n] = scores(sub, h, masked, n % slots)
            d = n - ATTN_AHEAD
            if d >= 0:
                sub, h, masked = work[d]
                update(sub, h, masked, d % slots, bmax.pop(d))

    @pl.when(kind == 0)
    def _():
        run([(sub, False) for sub in range(KV_SUB)])

    @pl.when(kind == 1)
    def _():
        run([(0, True)])

    @pl.when(kind == 2)
    def _():
        run([(0, False), (1, True)])

    @pl.when(kind != 0)
    def _():
        for h in range(FOX_HEADS):
            r0 = h * V_ROWS
            o_ref[h * FOX_DH:(h + 1) * FOX_DH, :] = (
                acc_ref[r0:r0 + FOX_DH, :] / acc_ref[r0 + FOX_DH:r0 + FOX_DH + 1, :])
        o = o_ref[...].T
        y_b = (o * sg_ref[0].astype(f32)).astype(bf16)
        merged = pam_ref[0].astype(f32) + gb_ref[0].astype(f32) * _dot(y_b, wb_ref[...])
        out = _dot(merged.astype(bf16), wo_ref[...])
        y_ref[0] = x_ref[0] + _rms(out, gpost_ref[...])


def _fox_attn(qt, k, vt, cref, sg, gb, pam, x, wb, wo, g_post):
    B, S, _ = x.shape
    T = min(TILE, S)
    nt = S // T
    assert nt % KV_SUB == 0
    steps = []
    for i in range(nt):
        for m in range(i // KV_SUB + 1):
            kind = 0 if m < i // KV_SUB else (1 if i % KV_SUB == 0 else 2)
            steps.append((i, m, kind))
    qi_tab, km_tab, kind_tab = (jnp.asarray([s[n] for s in steps], jnp.int32) for n in range(3))
    TK = KV_SUB * T
    full = lambda shape: pl.BlockSpec(shape, lambda b, s, qi, km, kind: (0,) * len(shape))
    qtile = pl.BlockSpec((1, T, D), lambda b, s, qi, km, kind: (b, qi[s], 0))
    grid_spec = pltpu.PrefetchScalarGridSpec(
        num_scalar_prefetch=3,
        grid=(B, len(steps)),
        in_specs=[pl.BlockSpec((1, FOX_HEADS * V_ROWS, T),
                               lambda b, s, qi, km, kind: (b, 0, qi[s])),
                  pl.BlockSpec((1, FOX_HEADS, TK, V_ROWS),
                               lambda b, s, qi, km, kind: (b, 0, km[s], 0)),
                  pl.BlockSpec((1, FOX_HEADS * V_ROWS, TK),
                               lambda b, s, qi, km, kind: (b, 0, km[s])),
                  pl.BlockSpec((1, KV_SUB, FOX_HEADS, T),
                               lambda b, s, qi, km, kind: (b, km[s], 0, 0)),
                  qtile, qtile, qtile, qtile, full((D, D)), full((D, D)), full((1, D))],
        out_specs=qtile,
        scratch_shapes=[pltpu.VMEM((FOX_HEADS * V_ROWS, T), f32),
                        pltpu.VMEM((FOX_HEADS, T), f32),
                        pltpu.VMEM((ATTN_AHEAD + 1, T, T), f32),
                        pltpu.VMEM((D, T), f32)],
    )
    return pl.pallas_call(
        _fox_attn_kernel,
        grid_spec=grid_spec,
        out_shape=jax.ShapeDtypeStruct((B, S, D), f32),
        compiler_params=pltpu.CompilerParams(
            dimension_semantics=("arbitrary", "arbitrary"), vmem_limit_bytes=VMEM_LIMIT),
        name="fox_attn",
    )(qi_tab, km_tab, kind_tab, qt, k, vt, cref, sg, gb, pam, x, wb, wo, g_post)


def _block_diag_groups(w):
    nb, bw, _ = w.shape
    per = nb // LRU_GROUPS
    eye = jnp.eye(per, dtype=w.dtype)
    blocks = w.reshape(LRU_GROUPS, per, bw, 1, bw) * eye[None, :, None, :, None]
    return blocks.reshape(LRU_GROUPS, per * bw, per * bw)


def kernel(x, mem, g_pre, w_in, conv_w, conv_b, w_lru_r, b_lru_r, w_lru_i, b_lru_i, lru_lambda,
           b_forget, g_mem, w_mem_k, w_mem_v, w_branch, b_merge, w_out, g_post):
    row = lambda v: v.reshape(1, -1).astype(f32)
    sizes = (D, D, D, D, D, FOX_HEADS, D, D, D, 3 * D)
    offs = np.concatenate([[0], np.cumsum(sizes)])
    cols = [w_in[:, offs[n]:offs[n + 1]] for n in range(len(sizes))]
    w_ax, w_ag, w_fq, w_fk, w_fv, w_ff, w_fg, w_mq, w_mg, w_merge = cols
    w_mga, w_mgb, w_mgc = (w_merge[:, n * D:(n + 1) * D].astype(bf16) for n in range(3))
    w_ff = jnp.pad(w_ff, ((0, 0), (0, LANES - FOX_HEADS))).astype(bf16)
    b_f = jnp.pad(b_forget, (0, LANES - FOX_HEADS)).reshape(1, LANES).astype(f32)
    wb = w_branch.astype(bf16)
    g_pre_r = row(g_pre)

    mk, mv = _mem_kv(mem, row(g_mem), w_mem_k.astype(bf16), w_mem_v.astype(bf16))
    pa = _lru(x, g_pre_r, w_ax.astype(bf16), w_ag.astype(bf16), w_mga, conv_w.astype(f32),
              row(conv_b), _block_diag_groups(w_lru_r).astype(bf16),
              _block_diag_groups(w_lru_i).astype(bf16), row(b_lru_r), row(b_lru_i),
              row(lru_lambda), wb[0], row(b_merge[0]))
    pam = _memattn(x, g_pre_r, w_mq.astype(bf16), w_mg.astype(bf16), w_mgc, mk, mv, wb[2],
                   row(b_merge[2]), pa)
    qt, k, vt, sg, gb, cref = _fox_proj(
        x, g_pre_r, w_fq.T.astype(bf16), w_fk.astype(bf16), w_fv.T.astype(bf16), w_ff, b_f,
        w_fg.astype(bf16), w_mgb, row(b_merge[1]))
    return _fox_attn(qt, k, vt, cref, sg, gb, pam, x, wb[1], w_out.astype(bf16), row(g_post))
```

```python
import functools
import math

import numpy as np
import jax
import jax.numpy as jnp
from jax import lax
from jax.experimental import pallas as pl
from jax.experimental.pallas import tpu as pltpu

D = 1024
TILE = 512
FOX_HEADS = 16
FOX_DH = 64
MEM_HEADS = 4
MEM_DH = 256
LRU_GROUPS = 4
LRU_GW = D // LRU_GROUPS
LRU_C = 8.0
LRU_CHUNKS = 2
CONV_W = 4
RMS_EPS = 1e-6
NEG_INF = -1e30
LOG2E = math.log2(math.e)
V_ROWS = FOX_DH + 16
ATTN_AHEAD = 2
KV_SUB = 2
LANES = 128
SUBLANES = 8
VMEM_LIMIT = 56 * 1024 * 1024

f32 = jnp.float32
bf16 = jnp.bfloat16


def _dot(a, b):
    return jnp.dot(a, b, preferred_element_type=f32)


def _dot_nt(a, b):
    return lax.dot_general(a, b, (((1,), (1,)), ((), ())), preferred_element_type=f32)


def _rms(x, g):
    return x * lax.rsqrt(jnp.mean(x * x, axis=-1, keepdims=True) + RMS_EPS) * g


def _sigmoid(z):
    return 1.0 / (1.0 + jnp.exp2(z * -LOG2E))


def _softplus(z):
    return jnp.maximum(z, 0.0) + jnp.log1p(jnp.exp(-jnp.abs(z)))


def _split3(v):
    hi = v.astype(bf16)
    r1 = v - hi.astype(f32)
    mid = r1.astype(bf16)
    lo = (r1 - mid.astype(f32)).astype(bf16)
    return hi, mid, lo


def _mem_kv_kernel(mem_ref, g_ref, wk_ref, wv_ref, mk_ref, mv_ref):
    mn = _rms(mem_ref[0], g_ref[...]).astype(bf16)
    mk_ref[0] = _dot(mn, wk_ref[...]).astype(bf16)
    mv_ref[0] = _dot(mn, wv_ref[...]).astype(bf16)


def _mem_kv(mem, g_mem, wk, wv):
    B, M, _ = mem.shape
    full = lambda shape: pl.BlockSpec(shape, lambda b: (0,) * len(shape))
    return pl.pallas_call(
        _mem_kv_kernel,
        grid=(B,),
        in_specs=[pl.BlockSpec((1, M, D), lambda b: (b, 0, 0)), full((1, D)),
                  full((D, D)), full((D, D))],
        out_specs=[pl.BlockSpec((1, M, D), lambda b: (b, 0, 0))] * 2,
        out_shape=[jax.ShapeDtypeStruct((B, M, D), bf16)] * 2,
        compiler_params=pltpu.CompilerParams(
            dimension_semantics=("arbitrary",), vmem_limit_bytes=VMEM_LIMIT),
        name="mem_kv",
    )(mem, g_mem, wk, wv)


def _lru_kernel(x_ref, g_ref, wax_ref, wag_ref, wmg_ref, cw_ref, cb_ref, wr_ref, wi_ref,
                br_ref, bi_ref, lam_ref, wb_ref, bm_ref, pa_ref,
                xn_ref, xpad_ref, h_ref, hc_ref):
    NB, _, TT, _ = x_ref.shape
    R = NB * TT
    halo = (CONV_W - 1) * NB
    t = pl.program_id(1)

    @pl.when(t == 0)
    def _():
        xpad_ref[0:halo, :] = jnp.zeros((halo, D), f32)
        hc_ref[...] = jnp.zeros((NB, D), f32)

    x = jnp.swapaxes(x_ref[:, 0], 0, 1).reshape(R, D)
    xn_ref[...] = _rms(x, g_ref[...]).astype(bf16)

    RC = R // LRU_CHUNKS
    TC = TT // LRU_CHUNKS
    softplus_lam = _softplus(-lam_ref[...])

    def rows_of(c):
        return slice(c * RC, (c + 1) * RC)

    def conv_in(c):
        ax = _dot(xn_ref[rows_of(c), :], wax_ref[...])
        xpad_ref[halo + c * RC:halo + (c + 1) * RC, :] = ax

    def gates(c):
        xc = cb_ref[...] + cw_ref[0:1, :] * xpad_ref[c * RC:(c + 1) * RC, :]
        for k in range(1, CONV_W):
            xc = xc + cw_ref[k:k + 1, :] * xpad_ref[c * RC + k * NB:(c + 1) * RC + k * NB, :]
        xcb = xc.astype(bf16)
        r = jnp.concatenate(
            [_dot(xcb[:, g * LRU_GW:(g + 1) * LRU_GW], wr_ref[g]) for g in range(LRU_GROUPS)],
            axis=1)
        i = jnp.concatenate(
            [_dot(xcb[:, g * LRU_GW:(g + 1) * LRU_GW], wi_ref[g]) for g in range(LRU_GROUPS)],
            axis=1)
        return xc, r, i

    def recurrence(c, xc, r, i, h):
        r = _sigmoid(r + br_ref[...])
        i = _sigmoid(i + bi_ref[...])
        log_a = (-LRU_C * r) * softplus_lam
        a = jnp.exp(log_a)
        z = -jnp.tanh(log_a) * (a * a + 1.0)
        u = jnp.where(z > 0.0, z * lax.rsqrt(z), 0.0) * (i * xc)
        for s in range(TC):
            rows = slice(s * NB, (s + 1) * NB)
            h = a[rows, :] * h + u[rows, :]
            h_ref[c * RC + s * NB:c * RC + (s + 1) * NB, :] = h
        return h

    def project(c, ag, mg):
        y_a = (h_ref[rows_of(c), :] * (ag * _sigmoid(ag))).astype(bf16)
        pa = _sigmoid(mg + bm_ref[...]) * _dot(y_a, wb_ref[...])
        pa_ref[:, 0, c * TC:(c + 1) * TC, :] = (
            jnp.swapaxes(pa.reshape(TC, NB, D), 0, 1).astype(bf16))

    chunks = range(LRU_CHUNKS)
    for c in chunks:
        conv_in(c)
    staged = []
    for c in chunks:
        ag = _dot(xn_ref[rows_of(c), :], wag_ref[...])
        staged.append((ag,) + gates(c))
    xpad_ref[0:halo, :] = xpad_ref[R:R + halo, :]
    mgs = [_dot(xn_ref[rows_of(c), :], wmg_ref[...]) for c in chunks]
    h = hc_ref[...]
    for c in chunks:
        ag, xc, r, i = staged[c]
        h = recurrence(c, xc, r, i, h)
        project(c, ag, mgs[c])
    hc_ref[...] = h


def _lru(x, g_pre, wax, wag, wmg, conv_w, conv_b, wr, wi, br, bi, lam, wb, bm):
    B, S, _ = x.shape
    NB = SUBLANES
    assert B % NB == 0
    TT = min(TILE // NB, S)
    full = lambda shape: pl.BlockSpec(shape, lambda g, t: (0,) * len(shape))
    tile = pl.BlockSpec((NB, 1, TT, D), lambda g, t: (g, t, 0, 0))
    pa = pl.pallas_call(
        _lru_kernel,
        grid=(B // NB, S // TT),
        in_specs=[tile, full((1, D)), full((D, D)), full((D, D)), full((D, D)),
                  full((CONV_W, D)), full((1, D)),
                  full((LRU_GROUPS, LRU_GW, LRU_GW)), full((LRU_GROUPS, LRU_GW, LRU_GW)),
                  full((1, D)), full((1, D)), full((1, D)), full((D, D)), full((1, D))],
        out_specs=tile,
        out_shape=jax.ShapeDtypeStruct((B, S // TT, TT, D), bf16),
        scratch_shapes=[pltpu.VMEM((TT * NB, D), bf16),
                        pltpu.VMEM(((TT + CONV_W - 1) * NB, D), f32), pltpu.VMEM((TT * NB, D), f32),
                        pltpu.VMEM((NB, D), f32)],
        compiler_params=pltpu.CompilerParams(
            dimension_semantics=("arbitrary", "arbitrary"), vmem_limit_bytes=VMEM_LIMIT),
        name="lru",
    )(x.reshape(B, S // TT, TT, D), g_pre, wax, wag, wmg, conv_w, conv_b, wr, wi, br, bi, lam,
      wb, bm)
    return pa.reshape(B, S, D)


def _memattn_kernel(x_ref, g_ref, wq_ref, wg_ref, wmg_ref, mk_ref, mv_ref, wb_ref, bm_ref,
                    pa_ref, out_ref):
    xn = _rms(x_ref[0], g_ref[...]).astype(bf16)
    q = (_dot(xn, wq_ref[...]) * (1.0 / math.sqrt(MEM_DH))).astype(bf16)
    outs = []
    for h in range(MEM_HEADS):
        sl = slice(h * MEM_DH, (h + 1) * MEM_DH)
        s = _dot_nt(q[:, sl], mk_ref[0, :, sl])
        p = jnp.exp(s - jnp.max(s, axis=-1, keepdims=True))
        l = jnp.sum(p, axis=-1, keepdims=True)
        outs.append(_dot(p.astype(bf16), mv_ref[0, :, sl]) / l)
    o = jnp.concatenate(outs, axis=1)
    mg = _dot(xn, wg_ref[...])
    y_m = (o * (mg * _sigmoid(mg))).astype(bf16)
    gate = _sigmoid(_dot(xn, wmg_ref[...]) + bm_ref[...])
    out_ref[0] = (pa_ref[0].astype(f32) + gate * _dot(y_m, wb_ref[...])).astype(bf16)


def _memattn(x, g_pre, wq, wg, wmg, mk, mv, wb, bm, pa):
    B, S, _ = x.shape
    M = mk.shape[1]
    T = min(TILE, S)
    full = lambda shape: pl.BlockSpec(shape, lambda b, t: (0,) * len(shape))
    tile = pl.BlockSpec((1, T, D), lambda b, t: (b, t, 0))
    memb = pl.BlockSpec((1, M, D), lambda b, t: (b, 0, 0))
    return pl.pallas_call(
        _memattn_kernel,
        grid=(B, S // T),
        in_specs=[tile, full((1, D)), full((D, D)), full((D, D)), full((D, D)), memb, memb,
                  full((D, D)), full((1, D)), tile],
        out_specs=tile,
        out_shape=jax.ShapeDtypeStruct((B, S, D), bf16),
        compiler_params=pltpu.CompilerParams(
            dimension_semantics=("arbitrary", "arbitrary"), vmem_limit_bytes=VMEM_LIMIT),
        name="memattn",
    )(x, g_pre, wq, wg, wmg, mk, mv, wb, bm, pa)


def _fox_proj_kernel(x_ref, g_ref, wqt_ref, wk_ref, wvt_ref, wf_ref, bf_ref, wfg_ref, wmg_ref,
                     bm_ref,
                     qt_ref, k_ref, vt_ref, sg_ref, gb_ref, cref_ref, carry_ref, d_ref):
    T = x_ref.shape[1]
    t = pl.program_id(1)

    @pl.when(t == 0)
    def _():
        carry_ref[...] = jnp.zeros((SUBLANES, LANES), f32)

    xn = _rms(x_ref[0], g_ref[...]).astype(bf16)

    lane = lax.broadcasted_iota(jnp.int32, (T, LANES), 1)
    z = _dot(xn, wf_ref[...]) + bf_ref[...]
    qt = (_dot_nt(wqt_ref[...], xn) * (LOG2E / math.sqrt(FOX_DH))).astype(bf16)
    ones = jnp.ones((V_ROWS - FOX_DH, T), bf16)
    for h in range(FOX_HEADS):
        qt_ref[0, h * V_ROWS:h * V_ROWS + FOX_DH, :] = qt[h * FOX_DH:(h + 1) * FOX_DH, :]
        qt_ref[0, h * V_ROWS + FOX_DH:(h + 1) * V_ROWS, :] = ones
    log_f = jnp.where(lane < FOX_HEADS, -LOG2E * _softplus(-z), 0.0)
    row = lax.broadcasted_iota(jnp.int32, (T, LANES), 0) % SUBLANES
    scan = log_f
    shift = 1
    while shift < SUBLANES:
        scan = scan + jnp.where(row >= shift, pltpu.roll(scan, shift, 0), 0.0)
        shift *= 2
    d_ref[...] = scan
    run = jnp.zeros((SUBLANES, LANES), f32)
    for grp in range(T // SUBLANES):
        rows = slice(grp * SUBLANES, (grp + 1) * SUBLANES)
        blk = d_ref[rows, :] + run
        d_ref[rows, :] = blk
        run = jnp.broadcast_to(blk[SUBLANES - 1:SUBLANES, :], (SUBLANES, LANES))
    d = d_ref[...]
    vt = _dot_nt(wvt_ref[...], xn).astype(bf16)
    for h in range(FOX_HEADS):
        vt_ref[0, h * V_ROWS:h * V_ROWS + FOX_DH, :] = vt[h * FOX_DH:(h + 1) * FOX_DH, :]
        vt_ref[0, h * V_ROWS + FOX_DH:(h + 1) * V_ROWS, :] = ones
    hi, mid, lo = (part.astype(f32) for part in _split3(-d))
    packed = []
    for g in range(4):
        mine = (lane < FOX_HEADS) & (lane % 4 == g)
        packed.append(jnp.where(mine, hi, 0.0) + pltpu.roll(jnp.where(mine, mid, 0.0), 1, 1)
                      + pltpu.roll(jnp.where(mine, lo, 0.0), 2, 1))
    bias_lanes = (lane >= FOX_DH) & (lane < FOX_DH + 3)
    kz = _dot(xn, wk_ref[...])
    for h in range(FOX_HEADS):
        pair = kz[:, (h // 2) * LANES:(h // 2 + 1) * LANES]
        if h % 2:
            pair = pltpu.roll(pair, FOX_DH, 1)
        terms = pltpu.roll(packed[h % 4], FOX_DH - h, 1)
        tile_h = jnp.where(lane < FOX_DH, pair, jnp.where(bias_lanes, terms, 0.0))
        k_ref[0, h] = tile_h[:, 0:V_ROWS].astype(bf16)

    carry = carry_ref[0:1, :]
    ident = (lax.broadcasted_iota(jnp.int32, (FOX_HEADS, LANES), 0)
             == lax.broadcasted_iota(jnp.int32, (FOX_HEADS, LANES), 1))
    col = jnp.sum(jnp.where(ident, jnp.broadcast_to(carry, (FOX_HEADS, LANES)), 0.0),
                  axis=1, keepdims=True)
    cref_ref[0, 0] = jnp.broadcast_to(col, (FOX_HEADS, T))
    carry_ref[...] = jnp.broadcast_to(carry + d[T - 1:T, :], (SUBLANES, LANES))

    fg = _dot(xn, wfg_ref[...])
    sg_ref[0] = (fg * _sigmoid(fg)).astype(bf16)
    gb_ref[0] = _sigmoid(_dot(xn, wmg_ref[...]) + bm_ref[...]).astype(bf16)


def _fox_proj(x, g_pre, wqt, wk, wvt, wf, bf, wfg, wmg, bm):
    B, S, _ = x.shape
    T = min(TILE, S)
    nt = S // T
    full = lambda shape: pl.BlockSpec(shape, lambda b, t: (0,) * len(shape))
    tile = pl.BlockSpec((1, T, D), lambda b, t: (b, t, 0))
    vrows = FOX_HEADS * V_ROWS
    return pl.pallas_call(
        _fox_proj_kernel,
        grid=(B, nt),
        in_specs=[tile, full((1, D)), full((D, D)), full((D, D)), full((D, D)),
                  full((D, LANES)), full((1, LANES)), full((D, D)), full((D, D)), full((1, D))],
        out_specs=[pl.BlockSpec((1, vrows, T), lambda b, t: (b, 0, t)),
                   pl.BlockSpec((1, FOX_HEADS, T, V_ROWS), lambda b, t: (b, 0, t, 0)),
                   pl.BlockSpec((1, vrows, T), lambda b, t: (b, 0, t)), tile, tile,
                   pl.BlockSpec((1, 1, FOX_HEADS, T), lambda b, t: (b, t, 0, 0))],
        out_shape=[jax.ShapeDtypeStruct((B, vrows, S), bf16),
                   jax.ShapeDtypeStruct((B, FOX_HEADS, S, V_ROWS), bf16),
                   jax.ShapeDtypeStruct((B, vrows, S), bf16),
                   jax.ShapeDtypeStruct((B, S, D), bf16),
                   jax.ShapeDtypeStruct((B, S, D), bf16),
                   jax.ShapeDtypeStruct((B, nt, FOX_HEADS, T), f32)],
        scratch_shapes=[pltpu.VMEM((SUBLANES, LANES), f32), pltpu.VMEM((T, LANES), f32)],
        compiler_params=pltpu.CompilerParams(
            dimension_semantics=("arbitrary", "arbitrary"), vmem_limit_bytes=VMEM_LIMIT),
        name="fox_proj",
    )(x, g_pre, wqt, wk, wvt, wf, bf, wfg, wmg, bm)


def _fox_attn_kernel(qi_ref, km_ref, kind_ref, qt_ref, k_ref, vt_ref, cref_ref, sg_ref,
                     gb_ref, pam_ref, x_ref, wb_ref, wo_ref, gpost_ref, y_ref,
                     acc_ref, m_ref, s_ref, o_ref):
    T = x_ref.shape[1]
    step = pl.program_id(1)
    kind = kind_ref[step]

    @pl.when(km_ref[step] == 0)
    def _():
        acc_ref[...] = jnp.zeros(acc_ref.shape, f32)
        m_ref[...] = jnp.full(m_ref.shape, NEG_INF, f32)

    def run(tiles):
        work = [(sub, h, masked) for sub, masked in tiles for h in range(FOX_HEADS)]
        H = T // 2
        causal = (lax.broadcasted_iota(jnp.int32, (H, T), 0)
                  <= lax.broadcasted_iota(jnp.int32, (H, T), 1))

        def logits(h, keys, queries):
            return _dot(k_ref[0, h, keys, :], qt_ref[0, h * V_ROWS:(h + 1) * V_ROWS, queries])

        def scores(sub, h, masked, slot):
            k0 = sub * T
            if not masked:
                st = logits(h, slice(k0, k0 + T), slice(0, T))
                s_ref[slot] = st
                return jnp.max(st, axis=0, keepdims=True)
            top = jnp.where(causal, logits(h, slice(k0, k0 + H), slice(0, T)), NEG_INF)
            low = jnp.where(causal[:, 0:H], logits(h, slice(k0 + H, k0 + T), slice(H, T)), NEG_INF)
            s_ref[slot, 0:H, :] = top
            s_ref[slot, H:T, H:T] = low
            top_max = jnp.max(top, axis=0, keepdims=True)
            low_max = jnp.max(low, axis=0, keepdims=True)
            return jnp.concatenate(
                [top_max[:, 0:H], jnp.maximum(top_max[:, H:T], low_max)], axis=1)

        def update(sub, h, masked, slot, bmax):
            cr = cref_ref[0, sub, h:h + 1, :]
            m_old = m_ref[h:h + 1, :]
            m_new = jnp.maximum(m_old, bmax - cr)
            alpha = jnp.exp2(m_old - m_new)
            shift = m_new + cr
            m_ref[h:h + 1, :] = m_new
            rows = slice(h * V_ROWS, (h + 1) * V_ROWS)
            k0 = sub * T
            if not masked:
                p = jnp.exp2(s_ref[slot] - shift).astype(bf16)
                pv = _dot(vt_ref[0, rows, k0:k0 + T], p)
                acc_ref[rows, :] = alpha * acc_ref[rows, :] + pv
                return
            p_top = jnp.exp2(s_ref[slot, 0:H, :] - shift).astype(bf16)
            p_low = jnp.exp2(s_ref[slot, H:T, H:T] - shift[:, H:T]).astype(bf16)
            pv = _dot(vt_ref[0, rows, k0:k0 + H], p_top)
            pv_low = _dot(vt_ref[0, rows, k0 + H:k0 + T], p_low)
            acc_ref[rows, 0:H] = alpha[:, 0:H] * acc_ref[rows, 0:H] + pv[:, 0:H]
            acc_ref[rows, H:T] = alpha[:, H:T] * acc_ref[rows, H:T] + (pv[:, H:T] + pv_low)

        slots = s_ref.shape[0]
        bmax = {}
        for n in range(len(work) + ATTN_AHEAD):
            if n < len(work):
                sub, h, masked = work[n]
                bmax[n] = scores(sub, h, masked, n % slots)
            d = n - ATTN_AHEAD
            if d >= 0:
                sub, h, masked = work[d]
                update(sub, h, masked, d % slots, bmax.pop(d))

    @pl.when(kind == 0)
    def _():
        run([(sub, False) for sub in range(KV_SUB)])

    @pl.when(kind == 1)
    def _():
        run([(0, True)])

    @pl.when(kind == 2)
    def _():
        run([(0, False), (1, True)])

    @pl.when(kind != 0)
    def _():
        for h in range(FOX_HEADS):
            r0 = h * V_ROWS
            inv_l = 1.0 / acc_ref[r0 + FOX_DH:r0 + FOX_DH + 1, :]
            o_ref[h * FOX_DH:(h + 1) * FOX_DH, :] = acc_ref[r0:r0 + FOX_DH, :] * inv_l
        o = o_ref[...].T
        y_b = (o * sg_ref[0].astype(f32)).astype(bf16)
        merged = pam_ref[0].astype(f32) + gb_ref[0].astype(f32) * _dot(y_b, wb_ref[...])
        out = _dot(merged.astype(bf16), wo_ref[...])
        y_ref[0] = x_ref[0] + _rms(out, gpost_ref[...])


def _fox_attn(qt, k, vt, cref, sg, gb, pam, x, wb, wo, g_post):
    B, S, _ = x.shape
    T = min(TILE, S)
    nt = S // T
    assert nt % KV_SUB == 0
    steps = []
    for i in range(nt):
        for m in range(i // KV_SUB + 1):
            kind = 0 if m < i // KV_SUB else (1 if i % KV_SUB == 0 else 2)
            steps.append((i, m, kind))
    qi_tab, km_tab, kind_tab = (jnp.asarray([s[n] for s in steps], jnp.int32) for n in range(3))
    TK = KV_SUB * T
    full = lambda shape: pl.BlockSpec(shape, lambda b, s, qi, km, kind: (0,) * len(shape))
    qtile = pl.BlockSpec((1, T, D), lambda b, s, qi, km, kind: (b, qi[s], 0))
    grid_spec = pltpu.PrefetchScalarGridSpec(
        num_scalar_prefetch=3,
        grid=(B, len(steps)),
        in_specs=[pl.BlockSpec((1, FOX_HEADS * V_ROWS, T),
                               lambda b, s, qi, km, kind: (b, 0, qi[s])),
                  pl.BlockSpec((1, FOX_HEADS, TK, V_ROWS),
                               lambda b, s, qi, km, kind: (b, 0, km[s], 0)),
                  pl.BlockSpec((1, FOX_HEADS * V_ROWS, TK),
                               lambda b, s, qi, km, kind: (b, 0, km[s])),
                  pl.BlockSpec((1, KV_SUB, FOX_HEADS, T),
                               lambda b, s, qi, km, kind: (b, km[s], 0, 0)),
                  qtile, qtile, qtile, qtile, full((D, D)), full((D, D)), full((1, D))],
        out_specs=qtile,
        scratch_shapes=[pltpu.VMEM((FOX_HEADS * V_ROWS, T), f32),
                        pltpu.VMEM((FOX_HEADS, T), f32),
                        pltpu.VMEM((ATTN_AHEAD + 1, T, T), f32),
                        pltpu.VMEM((D, T), f32)],
    )
    return pl.pallas_call(
        _fox_attn_kernel,
        grid_spec=grid_spec,
        out_shape=jax.ShapeDtypeStruct((B, S, D), f32),
        compiler_params=pltpu.CompilerParams(
            dimension_semantics=("arbitrary", "arbitrary"), vmem_limit_bytes=VMEM_LIMIT),
        name="fox_attn",
    )(qi_tab, km_tab, kind_tab, qt, k, vt, cref, sg, gb, pam, x, wb, wo, g_post)


def _block_diag_groups(w):
    nb, bw, _ = w.shape
    per = nb // LRU_GROUPS
    eye = jnp.eye(per, dtype=w.dtype)
    blocks = w.reshape(LRU_GROUPS, per, bw, 1, bw) * eye[None, :, None, :, None]
    return blocks.reshape(LRU_GROUPS, per * bw, per * bw)


def kernel(x, mem, g_pre, w_in, conv_w, conv_b, w_lru_r, b_lru_r, w_lru_i, b_lru_i, lru_lambda,
           b_forget, g_mem, w_mem_k, w_mem_v, w_branch, b_merge, w_out, g_post):
    row = lambda v: v.reshape(1, -1).astype(f32)
    sizes = (D, D, D, D, D, FOX_HEADS, D, D, D, 3 * D)
    offs = np.concatenate([[0], np.cumsum(sizes)])
    cols = [w_in[:, offs[n]:offs[n + 1]] for n in range(len(sizes))]
    w_ax, w_ag, w_fq, w_fk, w_fv, w_ff, w_fg, w_mq, w_mg, w_merge = cols
    w_mga, w_mgb, w_mgc = (w_merge[:, n * D:(n + 1) * D].astype(bf16) for n in range(3))
    w_ff = jnp.pad(w_ff, ((0, 0), (0, LANES - FOX_HEADS))).astype(bf16)
    b_f = jnp.pad(b_forget, (0, LANES - FOX_HEADS)).reshape(1, LANES).astype(f32)
    wb = w_branch.astype(bf16)
    g_pre_r = row(g_pre)

    mk, mv = _mem_kv(mem, row(g_mem), w_mem_k.astype(bf16), w_mem_v.astype(bf16))
    pa = _lru(x, g_pre_r, w_ax.astype(bf16), w_ag.astype(bf16), w_mga, conv_w.astype(f32),
              row(conv_b), _block_diag_groups(w_lru_r).astype(bf16),
              _block_diag_groups(w_lru_i).astype(bf16), row(b_lru_r), row(b_lru_i),
              row(lru_lambda), wb[0], row(b_merge[0]))
    pam = _memattn(x, g_pre_r, w_mq.astype(bf16), w_mg.astype(bf16), w_mgc, mk, mv, wb[2],
                   row(b_merge[2]), pa)
    qt, k, vt, sg, gb, cref = _fox_proj(
        x, g_pre_r, w_fq.T.astype(bf16), w_fk.astype(bf16), w_fv.T.astype(bf16), w_ff, b_f,
        w_fg.astype(bf16), w_mgb, row(b_merge[1]))
    return _fox_attn(qt, k, vt, cref, sg, gb, pam, x, wb[1], w_out.astype(bf16), row(g_post))
```
